```python
import math
import jax, jax.numpy as jnp
from jax import lax
import numpy as np

D_MODEL = 4096
BATCH = 4
SEQ = 2048
DEPTH = 2

HEAD_DIM = 128
MOBA_HEADS = 12
MOBA_WIDTH = MOBA_HEADS * HEAD_DIM
MOBA_BLOCK = 256
MOBA_TOPK = 3
MOBA_QCHUNK = 64
CONV_GROUPS = 8
CONV_WIDTH = CONV_GROUPS * HEAD_DIM
CONV_K = 3
GLA_HEADS = 12
GLA_DK = 64
GLA_DV = 128
GLA_KW = GLA_HEADS * GLA_DK
GLA_VW = GLA_HEADS * GLA_DV
GLA_RANK = 16
GLA_TAU = 16.0
GLA_CHUNK = 64
N_BRANCH = 3
D_FF = 11008
FFN_K = 3
NORM_EPS = 1e-6
IN_WIDTHS = (MOBA_WIDTH, MOBA_WIDTH, MOBA_WIDTH,
             CONV_WIDTH, CONV_WIDTH, CONV_WIDTH,
             GLA_KW, GLA_KW, GLA_VW, GLA_VW, GLA_RANK,
             N_BRANCH * D_MODEL)
IN_WIDTH = sum(IN_WIDTHS)

kernel_name = "hybrid_moba_conv_gla_adaln_trunk"


def split_points():
    return [int(p) for p in np.cumsum(np.array(IN_WIDTHS))[:-1]]


def alibi_slopes(n):
    def pow2(m):
        start = 2.0 ** (-(2.0 ** -(math.log2(m) - 3)))
        return [start ** (i + 1) for i in range(m)]
    if math.log2(n).is_integer():
        s = pow2(n)
    else:
        closest = 2 ** math.floor(math.log2(n))
        s = pow2(closest) + pow2(2 * closest)[0::2][: n - closest]
    return jnp.asarray(np.array(s, dtype=np.float32))


def rmsnorm(x, g):
    xf = x.astype(jnp.float32)
    y = xf * lax.rsqrt(jnp.mean(xf * xf, axis=-1, keepdims=True) + NORM_EPS)
    return (y * g.astype(jnp.float32)).astype(x.dtype)


def causal_dwconv(x, w):
    k_width = w.shape[0]
    s = x.shape[1]
    xp = jnp.pad(x, ((0, 0), (k_width - 1, 0), (0, 0)))
    y = w[k_width - 1] * x
    for k in range(k_width - 1):
        y = y + w[k] * xp[:, k:k + s]
    return y


def moba_attention(q, k, v, slopes):
    b, h, s, dh = q.shape
    nb = -(-s // MOBA_BLOCK)
    s_pad = nb * MOBA_BLOCK
    pad = ((0, 0), (0, 0), (0, s_pad - s), (0, 0))
    q, k, v = jnp.pad(q, pad), jnp.pad(k, pad), jnp.pad(v, pad)
    kb = k.reshape(b, h, nb, MOBA_BLOCK, dh)
    vb = v.reshape(b, h, nb, MOBA_BLOCK, dh)
    k_mean = jnp.mean(kb.astype(jnp.float32), axis=3)
    n_sel = min(MOBA_TOPK, nb - 1)
    scale = dh ** -0.5
    n_chunks = s_pad // MOBA_QCHUNK
    ratio = MOBA_BLOCK // MOBA_QCHUNK
    q_chunks = q.reshape(b, h, n_chunks, MOBA_QCHUNK, dh).transpose(2, 0, 1, 3, 4)
    bi = jnp.arange(b)[:, None, None, None]
    hi = jnp.arange(h)[None, :, None, None]
    offs = jnp.arange(MOBA_BLOCK)

    def one_chunk(args):
        ci, qc = args
        qc = qc.astype(jnp.float32) * scale
        t = ci * MOBA_QCHUNK + jnp.arange(MOBA_QCHUNK)
        own = ci // ratio
        k_own = lax.dynamic_index_in_dim(kb, own, axis=2, keepdims=False)
        v_own = lax.dynamic_index_in_dim(vb, own, axis=2, keepdims=False)
        s_own = jnp.einsum('bhqd,bhld->bhql', qc, k_own.astype(jnp.float32))
        dist_own = t[:, None] - (own * MOBA_BLOCK + offs)[None, :]
        s_own = s_own - slopes[:, None, None] * dist_own
        s_own = jnp.where(dist_own >= 0, s_own, -jnp.inf)
        if n_sel > 0:
            gate = jnp.einsum('bhqd,bhnd->bhqn', qc, k_mean)
            gate = jnp.where(jnp.arange(nb) < own, gate, -jnp.inf)
            top_v, top_i = lax.top_k(gate, n_sel)
            valid = jnp.isfinite(top_v)
            k_sel = kb[bi, hi, top_i].astype(jnp.float32)
            v_sel = vb[bi, hi, top_i]
            s_past = jnp.einsum('bhqd,bhqjld->bhqjl', qc, k_sel)
            key_pos = top_i[..., None] * MOBA_BLOCK + offs
            dist_past = t[:, None, None] - key_pos
            s_past = s_past - slopes[:, None, None, None] * dist_past
            s_past = jnp.where(valid[..., None], s_past, -jnp.inf)
            n_past = n_sel * MOBA_BLOCK
            logits = jnp.concatenate(
                [s_past.reshape(b, h, MOBA_QCHUNK, n_past), s_own], axis=-1)
            p = jax.nn.softmax(logits, axis=-1)
            p_past = p[..., :n_past].reshape(b, h, MOBA_QCHUNK, n_sel, MOBA_BLOCK)
            p_own = p[..., n_past:]
            o = (jnp.einsum('bhqjl,bhqjld->bhqd', p_past, v_sel.astype(jnp.float32))
                 + jnp.einsum('bhql,bhld->bhqd', p_own, v_own.astype(jnp.float32)))
        else:
            p_own = jax.nn.softmax(s_own, axis=-1)
            o = jnp.einsum('bhql,bhld->bhqd', p_own, v_own.astype(jnp.float32))
        return o.astype(v.dtype)

    o = lax.map(one_chunk, (jnp.arange(n_chunks), q_chunks))
    o = o.transpose(1, 2, 0, 3, 4).reshape(b, h, s_pad, dh)
    return o[:, :, :s]


def gla_chunked(q, k, v, log_a):
    b, s, _ = q.shape
    n = s // GLA_CHUNK

    def chunk(t, d):
        return t.astype(jnp.float32).reshape(b, n, GLA_CHUNK, GLA_HEADS, d).transpose(1, 0, 3, 2, 4)

    qf, kf, g = chunk(q, GLA_DK), chunk(k, GLA_DK), chunk(log_a, GLA_DK)
    vf = chunk(v, GLA_DV)
    bcum = jnp.cumsum(g, axis=3)
    q_i = qf * jnp.exp(bcum)
    k_i = kf * jnp.exp(-bcum)
    causal = jnp.tril(jnp.ones((GLA_CHUNK, GLA_CHUNK), dtype=bool))
    att = jnp.where(causal, jnp.einsum('nbhid,nbhjd->nbhij', q_i, k_i), 0.0)
    o_intra = jnp.einsum('nbhij,nbhje->nbhie', att, vf)
    b_last = bcum[:, :, :, -1]
    k_dec = kf * jnp.exp(b_last[:, :, :, None] - bcum)
    kv = jnp.einsum('nbhjd,nbhje->nbhde', k_dec, vf)

    def step(state, inp):
        q_c, decay, kv_c = inp
        o_inter = jnp.einsum('bhid,bhde->bhie', q_c, state)
        state = jnp.exp(decay)[..., None] * state + kv_c
        return state, o_inter

    s0 = jnp.zeros((b, GLA_HEADS, GLA_DK, GLA_DV), jnp.float32)
    _, o_inter = lax.scan(step, s0, (q_i, b_last, kv))
    o = o_intra + o_inter
    return o.transpose(1, 0, 3, 2, 4).reshape(b, s, GLA_HEADS, GLA_DV).astype(v.dtype)


def hybrid_mixer(h, w_in, conv_w, gla_a2, gla_a_bias, gla_norm_g,
                 w_branch_a, w_branch_b, w_branch_c, w_out, slopes):
    bsz, s, _ = h.shape
    z = h @ w_in
    (qa, ka, va, bx, bg, cg, qc, kc, vc, rc, ac, gate_logits) = jnp.split(z, split_points(), axis=-1)

    def heads(t):
        return t.reshape(bsz, s, MOBA_HEADS, HEAD_DIM).transpose(0, 2, 1, 3)
    ya = moba_attention(heads(qa), heads(ka), heads(va), slopes)
    ya = ya.transpose(0, 2, 1, 3).reshape(bsz, s, MOBA_WIDTH)

    yb = bg * causal_dwconv(cg * bx, conv_w)

    log_a = jax.nn.log_sigmoid((ac @ gla_a2 + gla_a_bias).astype(jnp.float32)) / GLA_TAU
    oc = gla_chunked(qc * (GLA_DK ** -0.5), kc, vc, log_a)
    oc = rmsnorm(oc, gla_norm_g).reshape(bsz, s, GLA_VW)
    yc = jax.nn.silu(rc) * oc

    g = jax.nn.sigmoid(gate_logits).reshape(bsz, s, N_BRANCH, D_MODEL)
    merged = (g[:, :, 0] * (ya @ w_branch_a)
              + g[:, :, 1] * (yb @ w_branch_b)
              + g[:, :, 2] * (yc @ w_branch_c))
    return merged @ w_out


def conv_ffn(h, w_gate, w_up, conv_w, w_down):
    u = causal_dwconv(h @ w_gate, conv_w)
    return (jax.nn.silu(u) * (h @ w_up)) @ w_down


def setup_inputs(seed: int = 0) -> dict:
    key = jax.random.key(seed)
    ks = jax.random.split(key, 24)
    f32 = jnp.float32

    def nrm(k, shape, scale):
        return jax.random.normal(k, shape, f32) * scale

    L = DEPTH
    return {
        "x": nrm(ks[0], (BATCH, SEQ, D_MODEL), 1.0),
        "c": nrm(ks[1], (BATCH, D_MODEL), 1.0),
        "norm1_g": 1.0 + nrm(ks[2], (L, D_MODEL), 0.01),
        "w_ada": nrm(ks[3], (L, D_MODEL, 6 * D_MODEL), 0.5 * D_MODEL ** -0.5),
        "b_ada": nrm(ks[4], (L, 6 * D_MODEL), 0.01),
        "w_in": nrm(ks[5], (L, D_MODEL, IN_WIDTH), D_MODEL ** -0.5),
        "conv_w": nrm(ks[6], (L, CONV_K, CONV_WIDTH), CONV_K ** -0.5),
        "gla_a2": nrm(ks[7], (L, GLA_RANK, GLA_KW), GLA_RANK ** -0.5),
        "gla_a_bias": nrm(ks[8], (L, GLA_KW), 0.1),
        "gla_norm_g": 1.0 + nrm(ks[9], (L, GLA_DV), 0.01),
        "w_branch_a": nrm(ks[10], (L, MOBA_WIDTH, D_MODEL), MOBA_WIDTH ** -0.5),
        "w_branch_b": nrm(ks[11], (L, CONV_WIDTH, D_MODEL), CONV_WIDTH ** -0.5),
        "w_branch_c": nrm(ks[12], (L, GLA_VW, D_MODEL), GLA_VW ** -0.5),
        "w_out": nrm(ks[13], (L, D_MODEL, D_MODEL), D_MODEL ** -0.5),
        "norm2_g": 1.0 + nrm(ks[14], (L, D_MODEL), 0.01),
        "w_ffn_gate": nrm(ks[15], (L, D_MODEL, D_FF), D_MODEL ** -0.5),
        "w_ffn_up": nrm(ks[16], (L, D_MODEL, D_FF), D_MODEL ** -0.5),
        "ffn_conv_w": nrm(ks[17], (L, FFN_K, D_FF), FFN_K ** -0.5),
        "w_ffn_down": nrm(ks[18], (L, D_FF, D_MODEL), D_FF ** -0.5),
        "final_norm_g": 1.0 + nrm(ks[19], (D_MODEL,), 0.01),
    }


def reference(x, c, norm1_g, w_ada, b_ada, w_in, conv_w, gla_a2, gla_a_bias, gla_norm_g,
              w_branch_a, w_branch_b, w_branch_c, w_out, norm2_g, w_ffn_gate, w_ffn_up,
              ffn_conv_w, w_ffn_down, final_norm_g):
    slopes = alibi_slopes(MOBA_HEADS)
    c_act = jax.nn.silu(c)
    for i in range(DEPTH):
        mod = (c_act @ w_ada[i] + b_ada[i])[:, None, :]
        sh1, sc1, gt1, sh2, sc2, gt2 = jnp.split(mod, 6, axis=-1)
        h = rmsnorm(x, norm1_g[i]) * (1 + sc1) + sh1
        x = x + gt1 * hybrid_mixer(h, w_in[i], conv_w[i], gla_a2[i], gla_a_bias[i], gla_norm_g[i],
                                   w_branch_a[i], w_branch_b[i], w_branch_c[i], w_out[i], slopes)
        h = rmsnorm(x, norm2_g[i]) * (1 + sc2) + sh2
        x = x + gt2 * conv_ffn(h, w_ffn_gate[i], w_ffn_up[i], ffn_conv_w[i], w_ffn_down[i])
    return rmsnorm(x, final_norm_g)
```

```python
import functools
import math

import numpy as np
import jax
import jax.numpy as jnp
from jax import lax
from jax.experimental import pallas as pl
from jax.experimental.pallas import tpu as pltpu

F32 = jnp.float32
BF16 = jnp.bfloat16

D_MODEL = 4096
BATCH = 4
SEQ = 2048
DEPTH = 2
N_TOK = BATCH * SEQ

HEAD_DIM = 128
MOBA_HEADS = 12
MOBA_WIDTH = MOBA_HEADS * HEAD_DIM
MOBA_BLOCK = 256
MOBA_TOPK = 3
N_MOBA_BLOCKS = SEQ // MOBA_BLOCK
CONV_WIDTH = 1024
GLA_HEADS = 12
GLA_DK = 64
GLA_DV = 128
GLA_KW = GLA_HEADS * GLA_DK
GLA_VW = GLA_HEADS * GLA_DV
GLA_RANK = 16
GLA_TAU = 16.0
GLA_CHUNK = 64
D_FF = 11008
NORM_EPS = 1e-6

OFF_QA = 0
OFF_KA = OFF_QA + MOBA_WIDTH
OFF_VA = OFF_KA + MOBA_WIDTH
OFF_BX = OFF_VA + MOBA_WIDTH
OFF_BG = OFF_BX + CONV_WIDTH
OFF_CG = OFF_BG + CONV_WIDTH
OFF_QC = OFF_CG + CONV_WIDTH
OFF_KC = OFF_QC + GLA_KW
OFF_VC = OFF_KC + GLA_KW
OFF_RC = OFF_VC + GLA_VW
OFF_AC = OFF_RC + GLA_VW
OFF_GATE = OFF_AC + GLA_RANK
Z_WIDTH = OFF_AC

LANES = 128
VMEM_LIMIT = 56 * 1024 * 1024


def _params(sem):
    return pltpu.CompilerParams(dimension_semantics=sem, vmem_limit_bytes=VMEM_LIMIT)


def _dot(a, b):
    return jnp.dot(a, b, preferred_element_type=F32)


def _dot_nt(a, b):
    return lax.dot_general(a, b, (((1,), (1,)), ((), ())), preferred_element_type=F32)


def _dot_tn(a, b):
    return lax.dot_general(a, b, (((0,), (0,)), ((), ())), preferred_element_type=F32)


def _split_bf16(x, parts):
    out = []
    r = x
    for _ in range(parts):
        p = r.astype(BF16)
        out.append(p)
        r = r - p.astype(F32)
    return out


def _silu(x):
    return x * jax.nn.sigmoid(x)


def alibi_slopes(n):
    def pow2(m):
        start = 2.0 ** (-(2.0 ** -(math.log2(m) - 3)))
        return [start ** (i + 1) for i in range(m)]
    if math.log2(n).is_integer():
        s = pow2(n)
    else:
        closest = 2 ** math.floor(math.log2(n))
        s = pow2(closest) + pow2(2 * closest)[0::2][: n - closest]
    return jnp.asarray(np.array(s, dtype=np.float32))


def _ada_kernel(c_ref, w_ref, b_ref, o_ref):
    ca = _silu(c_ref[...]).astype(BF16)
    o_ref[0] = _dot(ca, w_ref[0].astype(BF16)) + b_ref[0]


def ada_modulation(c_pad, w_ada, b_ada):
    n_out = w_ada.shape[-1]
    tn = 512
    return pl.pallas_call(
        _ada_kernel,
        grid=(DEPTH, n_out // tn),
        in_specs=[
            pl.BlockSpec((8, D_MODEL), lambda l, j: (0, 0)),
            pl.BlockSpec((1, D_MODEL, tn), lambda l, j: (l, 0, j)),
            pl.BlockSpec((1, 1, tn), lambda l, j: (l, 0, j)),
        ],
        out_specs=pl.BlockSpec((1, 8, tn), lambda l, j: (l, 0, j)),
        out_shape=jax.ShapeDtypeStruct((DEPTH, 8, n_out), F32),
        compiler_params=_params(("arbitrary", "arbitrary")),
        name="ada_modulation",
    )(c_pad, w_ada, b_ada.reshape(DEPTH, 1, n_out))


def _norm_mod_kernel(x_ref, g_ref, sc_ref, sh_ref, o_ref):
    x = x_ref[0]
    y = x * lax.rsqrt(jnp.mean(x * x, axis=-1, keepdims=True) + NORM_EPS) * g_ref[...]
    o_ref[0] = (y * (1.0 + sc_ref[0]) + sh_ref[0]).astype(o_ref.dtype)


def _norm_kernel(x_ref, g_ref, o_ref):
    x = x_ref[0]
    y = x * lax.rsqrt(jnp.mean(x * x, axis=-1, keepdims=True) + NORM_EPS) * g_ref[...]
    o_ref[0] = y.astype(o_ref.dtype)


def norm_modulate(x, g, sc, sh):
    tm = 256
    return pl.pallas_call(
        _norm_mod_kernel,
        grid=(BATCH, SEQ // tm),
        in_specs=[
            pl.BlockSpec((1, tm, D_MODEL), lambda b, i: (b, i, 0)),
            pl.BlockSpec((1, D_MODEL), lambda b, i: (0, 0)),
            pl.BlockSpec((1, 1, D_MODEL), lambda b, i: (b, 0, 0)),
            pl.BlockSpec((1, 1, D_MODEL), lambda b, i: (b, 0, 0)),
        ],
        out_specs=pl.BlockSpec((1, tm, D_MODEL), lambda b, i: (b, i, 0)),
        out_shape=jax.ShapeDtypeStruct((BATCH, SEQ, D_MODEL), BF16),
        compiler_params=_params(("arbitrary", "arbitrary")),
        name="norm_modulate",
    )(x, g.reshape(1, D_MODEL), sc, sh)


def final_norm(x, g):
    tm = 256
    return pl.pallas_call(
        _norm_kernel,
        grid=(BATCH, SEQ // tm),
        in_specs=[
            pl.BlockSpec((1, tm, D_MODEL), lambda b, i: (b, i, 0)),
            pl.BlockSpec((1, D_MODEL), lambda b, i: (0, 0)),
        ],
        out_specs=pl.BlockSpec((1, tm, D_MODEL), lambda b, i: (b, i, 0)),
        out_shape=jax.ShapeDtypeStruct((BATCH, SEQ, D_MODEL), F32),
        compiler_params=_params(("arbitrary", "arbitrary")),
        name="final_norm",
    )(x, g.reshape(1, D_MODEL))


def _mm_kernel(a_ref, w_ref, o_ref):
    o_ref[...] = _dot(a_ref[...], w_ref[...]).astype(o_ref.dtype)


def matmul(a, w, out_dtype, tm, tn, name):
    m, k = a.shape
    n = w.shape[1]
    return pl.pallas_call(
        _mm_kernel,
        grid=(m // tm, n // tn),
        in_specs=[
            pl.BlockSpec((tm, k), lambda i, j: (i, 0)),
            pl.BlockSpec((k, tn), lambda i, j: (0, j)),
        ],
        out_specs=pl.BlockSpec((tm, tn), lambda i, j: (i, j)),
        out_shape=jax.ShapeDtypeStruct((m, n), out_dtype),
        compiler_params=_params(("arbitrary", "arbitrary")),
        name=name,
    )(a, w)


def _moba_kernel(slopes_ref, q_ref, k_ref, v_ref, o_ref,
                 kmh_ref, kml_ref, gate_ref, m_ref, l_ref, acc_ref):
    h = pl.program_id(1)
    qi = pl.program_id(2)
    slope = slopes_ref[h]
    scale = HEAD_DIM ** -0.5
    blk = MOBA_BLOCK

    @pl.when(qi == 0)
    def _():
        r = lax.broadcasted_iota(jnp.int32, (LANES, SEQ), 0)
        t = lax.broadcasted_iota(jnp.int32, (LANES, SEQ), 1)
        pm = jnp.where(lax.shift_right_logical(t, 8) == r, 1.0 / blk, 0.0).astype(BF16)
        km = _dot(pm, k_ref[0])
        hi, lo = _split_bf16(km, 2)
        kmh_ref[...] = hi
        kml_ref[...] = lo

    q_raw = q_ref[0]
    qs = (q_raw.astype(F32) * scale).astype(BF16)
    gate_ref[...] = (_dot_nt(q_raw, kmh_ref[...]) + _dot_nt(q_raw, kml_ref[...])) * scale

    row = lax.broadcasted_iota(jnp.int32, (blk, blk), 0)
    col = lax.broadcasted_iota(jnp.int32, (blk, blk), 1)
    srel = slope * (col - row).astype(F32)

    own = pl.multiple_of(qi * blk, blk)
    s = _dot_nt(qs, k_ref[0, pl.ds(own, blk), :]) + srel
    s = jnp.where(col <= row, s, -jnp.inf)
    m0 = jnp.max(s, axis=1, keepdims=True)
    p = jnp.exp(s - m0)
    m_ref[...] = m0
    l_ref[...] = jnp.sum(p, axis=1, keepdims=True)
    acc_ref[...] = _dot(p.astype(BF16), v_ref[0, pl.ds(own, blk), :])

    lane = lax.broadcasted_iota(jnp.int32, (blk, LANES), 1)

    def body(n, carry):
        gate = gate_ref[...]
        gn = jnp.sum(jnp.where(lane == n, gate, 0.0), axis=1, keepdims=True)
        tie = jnp.where(lane < n, 1.0, 0.0)
        beats = jnp.where(gate > gn, 1.0, jnp.where(gate == gn, tie, 0.0))
        beats = jnp.where(lane < qi, beats, 0.0)
        sel = jnp.sum(beats, axis=1, keepdims=True) < float(MOBA_TOPK)

        start = pl.multiple_of(n * blk, blk)
        off = slope * ((qi - n) * blk).astype(F32)
        s = _dot_nt(qs, k_ref[0, pl.ds(start, blk), :]) + (srel - off)
        s = jnp.where(sel, s, -jnp.inf)
        m_old = m_ref[...]
        m_new = jnp.maximum(m_old, jnp.max(s, axis=1, keepdims=True))
        alpha = jnp.exp(m_old - m_new)
        p = jnp.exp(s - m_new)
        m_ref[...] = m_new
        l_ref[...] = alpha * l_ref[...] + jnp.sum(p, axis=1, keepdims=True)
        acc_ref[...] = alpha * acc_ref[...] + _dot(p.astype(BF16), v_ref[0, pl.ds(start, blk), :])
        return carry

    lax.fori_loop(0, qi, body, 0)
    o_ref[0] = (acc_ref[...] / l_ref[...]).astype(o_ref.dtype)


def moba_attention(z3, slopes):
    blk = MOBA_BLOCK
    cq, ck, cv = OFF_QA // HEAD_DIM, OFF_KA // HEAD_DIM, OFF_VA // HEAD_DIM
    return pl.pallas_call(
        _moba_kernel,
        grid=(BATCH, MOBA_HEADS, N_MOBA_BLOCKS),
        in_specs=[
            pl.BlockSpec(memory_space=pltpu.SMEM),
            pl.BlockSpec((1, blk, HEAD_DIM), lambda b, h, i: (b, i, cq + h)),
            pl.BlockSpec((1, SEQ, HEAD_DIM), lambda b, h, i: (b, 0, ck + h)),
            pl.BlockSpec((1, SEQ, HEAD_DIM), lambda b, h, i: (b, 0, cv + h)),
        ],
        out_specs=pl.BlockSpec((1, blk, HEAD_DIM), lambda b, h, i: (b, i, h)),
        out_shape=jax.ShapeDtypeStruct((BATCH, SEQ, MOBA_WIDTH), BF16),
        scratch_shapes=[
            pltpu.VMEM((LANES, HEAD_DIM), BF16),
            pltpu.VMEM((LANES, HEAD_DIM), BF16),
            pltpu.VMEM((blk, LANES), F32),
            pltpu.VMEM((blk, 1), F32),
            pltpu.VMEM((blk, 1), F32),
            pltpu.VMEM((blk, HEAD_DIM), F32),
        ],
        compiler_params=_params(("arbitrary", "arbitrary", "arbitrary")),
        name="moba_attention",
    )(slopes, z3, z3, z3)


def _shift_rows(x, k):
    rolled = pltpu.roll(x, k, 0)
    row = lax.broadcasted_iota(jnp.int32, x.shape, 0)
    return jnp.where(row < k, 0.0, rolled)


def _gconv_kernel(bx_ref, bg_ref, cg_ref, w_ref, o_ref):
    xin = cg_ref[0].astype(F32) * bx_ref[0].astype(F32)
    w = w_ref[...]
    y = w[2:3] * xin + w[1:2] * _shift_rows(xin, 1) + w[0:1] * _shift_rows(xin, 2)
    o_ref[0] = (bg_ref[0].astype(F32) * y).astype(o_ref.dtype)


def gated_conv(z3, conv_w):
    tc = 256
    cx, cg_, cc = OFF_BX // tc, OFF_BG // tc, OFF_CG // tc
    return pl.pallas_call(
        _gconv_kernel,
        grid=(BATCH, CONV_WIDTH // tc),
        in_specs=[
            pl.BlockSpec((1, SEQ, tc), lambda b, j: (b, 0, cx + j)),
            pl.BlockSpec((1, SEQ, tc), lambda b, j: (b, 0, cg_ + j)),
            pl.BlockSpec((1, SEQ, tc), lambda b, j: (b, 0, cc + j)),
            pl.BlockSpec((3, tc), lambda b, j: (0, j)),
        ],
        out_specs=pl.BlockSpec((1, SEQ, tc), lambda b, j: (b, 0, j)),
        out_shape=jax.ShapeDtypeStruct((BATCH, SEQ, CONV_WIDTH), BF16),
        compiler_params=_params(("arbitrary", "arbitrary")),
        name="gated_conv",
    )(z3, z3, z3, conv_w)


def _gla_kernel(q_ref, k_ref, v_ref, r_ref, ac_ref, a2_ref, ab_ref, ng_ref, o_ref, st_ref):
    cl = GLA_CHUNK
    st_ref[...] = jnp.zeros_like(st_ref)

    lane = lax.broadcasted_iota(jnp.int32, (1, 2 * GLA_DK), 1)
    head0 = jnp.where(lane < GLA_DK, 1.0, 0.0)
    head1 = 1.0 - head0
    ri = lax.broadcasted_iota(jnp.int32, (cl, cl), 0)
    ci = lax.broadcasted_iota(jnp.int32, (cl, cl), 1)
    causal = ci <= ri
    tril = jnp.where(causal, 1.0, 0.0).astype(BF16)
    sr = lax.broadcasted_iota(jnp.int32, (2 * GLA_DV, 2 * GLA_DK), 0)
    sc = lax.broadcasted_iota(jnp.int32, (2 * GLA_DV, 2 * GLA_DK), 1)
    same_head = jnp.where((sr < GLA_DV) == (sc < GLA_DK), 1.0, 0.0)

    a2h, a2l = _split_bf16(a2_ref[...], 2)
    a_bias = ab_ref[...]
    ng = ng_ref[...]

    def body(c, carry):
        rows = pl.ds(pl.multiple_of(c * cl, cl), cl)
        ah, al = _split_bf16(ac_ref[0, rows, :], 2)
        zz = _dot(ah, a2h) + _dot(ah, a2l) + _dot(al, a2h) + a_bias
        g = (jnp.minimum(zz, 0.0) - jnp.log1p(jnp.exp(-jnp.abs(zz)))) * (1.0 / GLA_TAU)
        g0, g1, g2 = _split_bf16(g, 3)
        bcum = _dot(tril, g0) + _dot(tril, g1) + _dot(tril, g2)
        b_last = bcum[cl - 1:cl, :]
        qf = q_ref[0, rows, :].astype(F32) * (GLA_DK ** -0.5)
        kf = k_ref[0, rows, :].astype(F32)
        q_i = qf * jnp.exp(bcum)
        k_i = (kf * jnp.exp(-bcum)).astype(BF16)
        k_dec = (kf * jnp.exp(b_last - bcum)).astype(BF16)
        v = v_ref[0, rows, :]

        halves = []
        for hh, hm in enumerate((head0, head1)):
            att = _dot_nt((q_i * hm).astype(BF16), k_i)
            att = jnp.where(causal, att, 0.0).astype(BF16)
            halves.append(_dot(att, v[:, hh * GLA_DV:(hh + 1) * GLA_DV]))
        st = st_ref[...]
        o = jnp.concatenate(halves, axis=1) + _dot_nt(q_i.astype(BF16), st.astype(BF16))
        st_ref[...] = st * jnp.exp(b_last) + _dot_tn(v, k_dec) * same_head

        outs = []
        for hh in range(2):
            oh = o[:, hh * GLA_DV:(hh + 1) * GLA_DV]
            ms = jnp.mean(oh * oh, axis=-1, keepdims=True)
            outs.append(oh * lax.rsqrt(ms + NORM_EPS) * ng)
        y = jnp.concatenate(outs, axis=1)
        o_ref[0, rows, :] = (_silu(r_ref[0, rows, :].astype(F32)) * y).astype(o_ref.dtype)
        return carry

    lax.fori_loop(0, SEQ // cl, body, 0)


def gla_mixer(z3, ac3, a2_pad, a_bias, norm_g):
    kw, vw = 2 * GLA_DK, 2 * GLA_DV
    cq, ck, cv, cr = OFF_QC // kw, OFF_KC // kw, OFF_VC // vw, OFF_RC // vw
    return pl.pallas_call(
        _gla_kernel,
        grid=(BATCH, GLA_HEADS // 2),
        in_specs=[
            pl.BlockSpec((1, SEQ, kw), lambda b, j: (b, 0, cq + j)),
            pl.BlockSpec((1, SEQ, kw), lambda b, j: (b, 0, ck + j)),
            pl.BlockSpec((1, SEQ, vw), lambda b, j: (b, 0, cv + j)),
            pl.BlockSpec((1, SEQ, vw), lambda b, j: (b, 0, cr + j)),
            pl.BlockSpec((1, SEQ, LANES), lambda b, j: (b, 0, 0)),
            pl.BlockSpec((LANES, kw), lambda b, j: (0, j)),
            pl.BlockSpec((1, kw), lambda b, j: (0, j)),
            pl.BlockSpec((1, GLA_DV), lambda b, j: (0, 0)),
        ],
        out_specs=pl.BlockSpec((1, SEQ, vw), lambda b, j: (b, 0, j)),
        out_shape=jax.ShapeDtypeStruct((BATCH, SEQ, GLA_VW), BF16),
        scratch_shapes=[pltpu.VMEM((vw, kw), F32)],
        compiler_params=_params(("arbitrary", "arbitrary")),
        name="gla_mixer",
    )(z3, z3, z3, z3, ac3, a2_pad, a_bias.reshape(1, GLA_KW), norm_g.reshape(1, GLA_DV))


def _merge_kernel(h_ref, ya_ref, yb_ref, yc_ref, wg0_ref, wg1_ref, wg2_ref,
                  wa_ref, wb_ref, wc_ref, o_ref):
    h = h_ref[...]
    acc = jax.nn.sigmoid(_dot(h, wg0_ref[...])) * _dot(ya_ref[...], wa_ref[...])
    acc += jax.nn.sigmoid(_dot(h, wg1_ref[...])) * _dot(yb_ref[...], wb_ref[...])
    acc += jax.nn.sigmoid(_dot(h, wg2_ref[...])) * _dot(yc_ref[...], wc_ref[...])
    o_ref[...] = acc.astype(o_ref.dtype)


def merge_branches(h2, ya, yb, yc, w_gl, wa, wb, wc):
    tm, tn = 512, 256
    nj = D_MODEL // tn
    row = lambda i, j: (i, 0)
    colw = lambda i, j: (0, j)
    return pl.pallas_call(
        _merge_kernel,
        grid=(N_TOK // tm, nj),
        in_specs=[
            pl.BlockSpec((tm, D_MODEL), row),
            pl.BlockSpec((tm, MOBA_WIDTH), row),
            pl.BlockSpec((tm, CONV_WIDTH), row),
            pl.BlockSpec((tm, GLA_VW), row),
            pl.BlockSpec((D_MODEL, tn), lambda i, j: (0, j)),
            pl.BlockSpec((D_MODEL, tn), lambda i, j: (0, nj + j)),
            pl.BlockSpec((D_MODEL, tn), lambda i, j: (0, 2 * nj + j)),
            pl.BlockSpec((MOBA_WIDTH, tn), colw),
            pl.BlockSpec((CONV_WIDTH, tn), colw),
            pl.BlockSpec((GLA_VW, tn), colw),
        ],
        out_specs=pl.BlockSpec((tm, tn), lambda i, j: (i, j)),
        out_shape=jax.ShapeDtypeStruct((N_TOK, D_MODEL), BF16),
        compiler_params=_params(("arbitrary", "arbitrary")),
        name="merge_branches",
    )(h2, ya, yb, yc, w_gl, w_gl, w_gl, wa, wb, wc)


def _proj_residual_kernel(a_ref, w_ref, x_ref, gt_ref, o_ref):
    o_ref[...] = x_ref[...] + gt_ref[0] * _dot(a_ref[...], w_ref[...])


def proj_residual(a, w, x2, gt, tm, tn, name):
    m, k = a.shape
    n = w.shape[1]
    per_batch = SEQ // tm
    return pl.pallas_call(
        _proj_residual_kernel,
        grid=(m // tm, n // tn),
        in_specs=[
            pl.BlockSpec((tm, k), lambda i, j: (i, 0)),
            pl.BlockSpec((k, tn), lambda i, j: (0, j)),
            pl.BlockSpec((tm, tn), lambda i, j: (i, j)),
            pl.BlockSpec((1, 1, tn), lambda i, j: (i // per_batch, 0, j)),
        ],
        out_specs=pl.BlockSpec((tm, tn), lambda i, j: (i, j)),
        out_shape=jax.ShapeDtypeStruct((m, n), F32),
        compiler_params=_params(("arbitrary", "arbitrary")),
        name=name,
    )(a, w, x2, gt)


def _ffn_up_kernel(a_ref, wg_ref, wu_ref, cw_ref, o_ref, carry_ref, *, tiles_per_seq):
    i = pl.program_id(0)
    j = pl.program_id(1)
    a = a_ref[...]
    g = _dot(a, wg_ref[...])
    up = _dot(a, wu_ref[...])
    tm = g.shape[0]

    @pl.when(i % tiles_per_seq == 0)
    def _():
        carry_ref[j] = jnp.zeros(carry_ref.shape[1:], F32)

    prev = carry_ref[j]
    carry_ref[j] = g[tm - 8:tm, :]
    row = lax.broadcasted_iota(jnp.int32, g.shape, 0)
    s1 = jnp.where(row == 0, prev[7:8], pltpu.roll(g, 1, 0))
    s2 = jnp.where(row == 0, prev[6:7], jnp.where(row == 1, prev[7:8], pltpu.roll(g, 2, 0)))
    cw = cw_ref[...]
    u = cw[2:3] * g + cw[1:2] * s1 + cw[0:1] * s2
    o_ref[...] = (_silu(u) * up).astype(o_ref.dtype)


def ffn_up(h2, wg, wu, conv_w):
    tm, tn = 1024, 256
    nj = D_FF // tn
    return pl.pallas_call(
        functools.partial(_ffn_up_kernel, tiles_per_seq=SEQ // tm),
        grid=(N_TOK // tm, nj),
        in_specs=[
            pl.BlockSpec((tm, D_MODEL), lambda i, j: (i, 0)),
            pl.BlockSpec((D_MODEL, tn), lambda i, j: (0, j)),
            pl.BlockSpec((D_MODEL, tn), lambda i, j: (0, j)),
            pl.BlockSpec((3, tn), lambda i, j: (0, j)),
        ],
        out_specs=pl.BlockSpec((tm, tn), lambda i, j: (i, j)),
        out_shape=jax.ShapeDtypeStruct((N_TOK, D_FF), BF16),
        scratch_shapes=[pltpu.VMEM((nj, 8, tn), F32)],
        compiler_params=_params(("arbitrary", "arbitrary")),
        name="ffn_up",
    )(h2, wg, wu, conv_w)


def kernel(x, c, norm1_g, w_ada, b_ada, w_in, conv_w, gla_a2, gla_a_bias, gla_norm_g,
           w_branch_a, w_branch_b, w_branch_c, w_out, norm2_g, w_ffn_gate, w_ffn_up,
           ffn_conv_w, w_ffn_down, final_norm_g):
    slopes = alibi_slopes(MOBA_HEADS)
    c_pad = jnp.zeros((8, D_MODEL), F32).at[:BATCH].set(c)
    mod = ada_modulation(c_pad, w_ada, b_ada)
    mod = mod[:, :BATCH].reshape(DEPTH, BATCH, 6, 1, D_MODEL)

    for l in range(DEPTH):
        sh1, sc1, gt1, sh2, sc2, gt2 = (mod[l, :, t] for t in range(6))

        w_main = w_in[l, :, :Z_WIDTH].astype(BF16)
        w_ac = jnp.zeros((D_MODEL, LANES), BF16).at[:, :GLA_RANK].set(
            w_in[l, :, OFF_AC:OFF_GATE].astype(BF16))
        w_gl = w_in[l, :, OFF_GATE:].astype(BF16)
        a2_pad = jnp.zeros((LANES, GLA_KW), F32).at[:GLA_RANK].set(gla_a2[l])

        h = norm_modulate(x, norm1_g[l], sc1, sh1)
        h2 = h.reshape(N_TOK, D_MODEL)
        z = matmul(h2, w_main, BF16, 1024, 512, "in_proj")
        ac = matmul(h2, w_ac, F32, 1024, LANES, "in_proj_rank")
        z3 = z.reshape(BATCH, SEQ, Z_WIDTH)

        ya = moba_attention(z3, slopes)
        yb = gated_conv(z3, conv_w[l])
        yc = gla_mixer(z3, ac.reshape(BATCH, SEQ, LANES), a2_pad, gla_a_bias[l], gla_norm_g[l])

        merged = merge_branches(
            h2, ya.reshape(N_TOK, MOBA_WIDTH), yb.reshape(N_TOK, CONV_WIDTH),
            yc.reshape(N_TOK, GLA_VW), w_gl,
            w_branch_a[l].astype(BF16), w_branch_b[l].astype(BF16), w_branch_c[l].astype(BF16))
        x2 = proj_residual(merged, w_out[l].astype(BF16), x.reshape(N_TOK, D_MODEL), gt1,
                           1024, 512, "out_proj")
        x = x2.reshape(BATCH, SEQ, D_MODEL)

        h = norm_modulate(x, norm2_g[l], sc2, sh2)
        act = ffn_up(h.reshape(N_TOK, D_MODEL), w_ffn_gate[l].astype(BF16),
                     w_ffn_up[l].astype(BF16), ffn_conv_w[l])
        x2 = proj_residual(act, w_ffn_down[l].astype(BF16), x2, gt2, 512, 256, "ffn_down")
        x = x2.reshape(BATCH, SEQ, D_MODEL)

    return final_norm(x, final_norm_g)
```

```python
import functools
import math

import numpy as np
import jax
import jax.numpy as jnp
from jax import lax
from jax.experimental import pallas as pl
from jax.experimental.pallas import tpu as pltpu

F32 = jnp.float32
BF16 = jnp.bfloat16

D_MODEL = 4096
SEQ = 2048
DEPTH = 2

HEAD_DIM = 128
MOBA_HEADS = 12
MOBA_WIDTH = MOBA_HEADS * HEAD_DIM
MOBA_BLOCK = 256
MOBA_TOPK = 3
N_MOBA_BLOCKS = SEQ // MOBA_BLOCK
CONV_WIDTH = 1024
GLA_HEADS = 12
GLA_DK = 64
GLA_DV = 128
GLA_KW = GLA_HEADS * GLA_DK
GLA_VW = GLA_HEADS * GLA_DV
GLA_RANK = 16
GLA_TAU = 16.0
GLA_CHUNK = 64
D_FF = 11008
NORM_EPS = 1e-6

OFF_QA = 0
OFF_KA = OFF_QA + MOBA_WIDTH
OFF_VA = OFF_KA + MOBA_WIDTH
OFF_BX = OFF_VA + MOBA_WIDTH
OFF_BG = OFF_BX + CONV_WIDTH
OFF_CG = OFF_BG + CONV_WIDTH
OFF_QC = OFF_CG + CONV_WIDTH
OFF_KC = OFF_QC + GLA_KW
OFF_VC = OFF_KC + GLA_KW
OFF_RC = OFF_VC + GLA_VW
OFF_AC = OFF_RC + GLA_VW
OFF_GATE = OFF_AC + GLA_RANK
Z_WIDTH = OFF_AC

LANES = 128
VMEM_LIMIT = 56 * 1024 * 1024
RESIDENT = pl.Buffered(1)


def _params(sem):
    return pltpu.CompilerParams(dimension_semantics=sem, vmem_limit_bytes=VMEM_LIMIT)


def _dot(a, b):
    return jnp.dot(a, b, preferred_element_type=F32)


def _dot_nt(a, b):
    return lax.dot_general(a, b, (((1,), (1,)), ((), ())), preferred_element_type=F32)


def _dot_tn(a, b):
    return lax.dot_general(a, b, (((0,), (0,)), ((), ())), preferred_element_type=F32)


def _split_bf16(x, parts):
    out = []
    r = x
    for _ in range(parts):
        p = r.astype(BF16)
        out.append(p)
        r = r - p.astype(F32)
    return out


def _silu(x):
    return x * jax.nn.sigmoid(x)


def alibi_slopes(n):
    def pow2(m):
        start = 2.0 ** (-(2.0 ** -(math.log2(m) - 3)))
        return [start ** (i + 1) for i in range(m)]
    if math.log2(n).is_integer():
        s = pow2(n)
    else:
        closest = 2 ** math.floor(math.log2(n))
        s = pow2(closest) + pow2(2 * closest)[0::2][: n - closest]
    return jnp.asarray(np.array(s, dtype=np.float32))


def _ada_kernel(c_ref, w_ref, b_ref, o_ref):
    ca = _silu(c_ref[...]).astype(BF16)
    o_ref[0] = _dot(ca, w_ref[0].astype(BF16)) + b_ref[0]


def ada_modulation(c_pad, w_ada, b_ada):
    n_out = w_ada.shape[-1]
    tn = 512
    return pl.pallas_call(
        _ada_kernel,
        grid=(DEPTH, n_out // tn),
        in_specs=[
            pl.BlockSpec((8, D_MODEL), lambda l, j: (0, 0)),
            pl.BlockSpec((1, D_MODEL, tn), lambda l, j: (l, 0, j)),
            pl.BlockSpec((1, 1, tn), lambda l, j: (l, 0, j)),
        ],
        out_specs=pl.BlockSpec((1, 8, tn), lambda l, j: (l, 0, j)),
        out_shape=jax.ShapeDtypeStruct((DEPTH, 8, n_out), F32),
        compiler_params=_params(("arbitrary", "arbitrary")),
        name="ada_modulation",
    )(c_pad, w_ada, b_ada.reshape(DEPTH, 1, n_out))


def _norm_mod_kernel(x_ref, g_ref, sc_ref, sh_ref, o_ref):
    x = x_ref[0]
    y = x * lax.rsqrt(jnp.mean(x * x, axis=-1, keepdims=True) + NORM_EPS) * g_ref[...]
    o_ref[0] = (y * (1.0 + sc_ref[0]) + sh_ref[0]).astype(o_ref.dtype)


def _norm_kernel(x_ref, g_ref, o_ref):
    x = x_ref[0]
    y = x * lax.rsqrt(jnp.mean(x * x, axis=-1, keepdims=True) + NORM_EPS) * g_ref[...]
    o_ref[0] = y.astype(o_ref.dtype)


def norm_modulate(x, g, sc, sh):
    tm = 256
    batch = x.shape[0]
    return pl.pallas_call(
        _norm_mod_kernel,
        grid=(batch, SEQ // tm),
        in_specs=[
            pl.BlockSpec((1, tm, D_MODEL), lambda b, i: (b, i, 0)),
            pl.BlockSpec((1, D_MODEL), lambda b, i: (0, 0)),
            pl.BlockSpec((1, 1, D_MODEL), lambda b, i: (b, 0, 0)),
            pl.BlockSpec((1, 1, D_MODEL), lambda b, i: (b, 0, 0)),
        ],
        out_specs=pl.BlockSpec((1, tm, D_MODEL), lambda b, i: (b, i, 0)),
        out_shape=jax.ShapeDtypeStruct((batch, SEQ, D_MODEL), BF16),
        compiler_params=_params(("arbitrary", "arbitrary")),
        name="norm_modulate",
    )(x, g.reshape(1, D_MODEL), sc, sh)


def final_norm(x, g):
    tm = 256
    batch = x.shape[0]
    return pl.pallas_call(
        _norm_kernel,
        grid=(batch, SEQ // tm),
        in_specs=[
            pl.BlockSpec((1, tm, D_MODEL), lambda b, i: (b, i, 0)),
            pl.BlockSpec((1, D_MODEL), lambda b, i: (0, 0)),
        ],
        out_specs=pl.BlockSpec((1, tm, D_MODEL), lambda b, i: (b, i, 0)),
        out_shape=jax.ShapeDtypeStruct((batch, SEQ, D_MODEL), F32),
        compiler_params=_params(("arbitrary", "arbitrary")),
        name="final_norm",
    )(x, g.reshape(1, D_MODEL))


def _mm_kernel(a_ref, w_ref, o_ref):
    o_ref[...] = _dot(a_ref[...], w_ref[...].astype(BF16)).astype(o_ref.dtype)


def in_proj(a, w_in, layer, col0, n, out_dtype, tm, tn, name):
    m, k = a.shape
    j0 = col0 // tn
    return pl.pallas_call(
        _mm_kernel,
        grid=(m // tm, n // tn),
        in_specs=[
            pl.BlockSpec((tm, k), lambda i, j: (i, 0), pipeline_mode=RESIDENT),
            pl.BlockSpec((None, k, tn), lambda i, j: (layer, 0, j0 + j)),
        ],
        out_specs=pl.BlockSpec((tm, tn), lambda i, j: (i, j)),
        out_shape=jax.ShapeDtypeStruct((m, n), out_dtype),
        compiler_params=_params(("arbitrary", "arbitrary")),
        name=name,
    )(a, w_in)


def _moba_kernel(slopes_ref, q_ref, k_ref, v_ref, o_ref, s_ref):
    slope = slopes_ref[pl.program_id(1)]
    scale = HEAD_DIM ** -0.5
    blk = MOBA_BLOCK

    r = lax.broadcasted_iota(jnp.int32, (LANES, SEQ), 0)
    t = lax.broadcasted_iota(jnp.int32, (LANES, SEQ), 1)
    pm = jnp.where(lax.shift_right_logical(t, 8) == r, 1.0 / blk, 0.0).astype(BF16)
    kmh, kml = _split_bf16(_dot(pm, k_ref[0]), 2)

    row = lax.broadcasted_iota(jnp.int32, (blk, blk), 0)
    col = lax.broadcasted_iota(jnp.int32, (blk, blk), 1)
    srel = slope * (col - row).astype(F32)
    causal = col <= row
    eye = jnp.where(row == col, 1.0, 0.0).astype(BF16)
    blk_id = lax.broadcasted_iota(jnp.int32, (N_MOBA_BLOCKS, blk), 0)

    for qi in range(N_MOBA_BLOCKS):
        rows = slice(qi * blk, (qi + 1) * blk)
        q_raw = q_ref[0, rows, :]
        qs = (q_raw.astype(F32) * scale).astype(BF16)

        sel_b = None
        if qi > MOBA_TOPK:
            g = (_dot_nt(kmh, q_raw) + _dot_nt(kml, q_raw))[:N_MOBA_BLOCKS] * scale
            rank = jnp.zeros_like(g)
            for m in range(qi):
                gm = g[m:m + 1, :]
                tie = jnp.where(blk_id > m, 1.0, 0.0)
                rank = rank + jnp.where(gm > g, 1.0, jnp.where(gm == g, tie, 0.0))
            sel = jnp.where(rank < float(MOBA_TOPK), 1.0, 0.0)
            sel_b = [_dot_nt(eye, jnp.broadcast_to(sel[n:n + 1, :], (LANES, blk)).astype(BF16))
                     for n in range(qi)]

        m_run = None
        for n in range(qi + 1):
            cols = slice(n * blk, (n + 1) * blk)
            s = _dot_nt(qs, k_ref[0, cols, :]) + srel
            if n == qi:
                s = jnp.where(causal, s, -jnp.inf)
            s_ref[:, cols] = s
            rm = jnp.max(s, axis=1, keepdims=True) - slope * float(blk * (qi - n))
            if sel_b is not None and n < qi:
                rm = jnp.where(sel_b[n][:, 0:1] > 0.5, rm, -jnp.inf)
            m_run = rm if m_run is None else jnp.maximum(m_run, rm)

        l_run = None
        acc = None
        for n in range(qi + 1):
            cols = slice(n * blk, (n + 1) * blk)
            p = jnp.exp(s_ref[:, cols] - (m_run + slope * float(blk * (qi - n))))
            rs = jnp.sum(p, axis=1, keepdims=True)
            pv = _dot(p.astype(BF16), v_ref[0, cols, :])
            if sel_b is not None and n < qi:
                rs = jnp.where(sel_b[n][:, 0:1] > 0.5, rs, 0.0)
                pv = jnp.where(sel_b[n] > 0.5, pv, 0.0)
            l_run = rs if l_run is None else l_run + rs
            acc = pv if acc is None else acc + pv
        o_ref[0, rows, :] = (acc / l_run).astype(o_ref.dtype)


def moba_attention(z3, slopes):
    batch = z3.shape[0]
    cq, ck, cv = OFF_QA // HEAD_DIM, OFF_KA // HEAD_DIM, OFF_VA // HEAD_DIM
    return pl.pallas_call(
        _moba_kernel,
        grid=(batch, MOBA_HEADS),
        in_specs=[
            pl.BlockSpec(memory_space=pltpu.SMEM),
            pl.BlockSpec((1, SEQ, HEAD_DIM), lambda b, h: (b, 0, cq + h)),
            pl.BlockSpec((1, SEQ, HEAD_DIM), lambda b, h: (b, 0, ck + h)),
            pl.BlockSpec((1, SEQ, HEAD_DIM), lambda b, h: (b, 0, cv + h)),
        ],
        out_specs=pl.BlockSpec((1, SEQ, HEAD_DIM), lambda b, h: (b, 0, h)),
        out_shape=jax.ShapeDtypeStruct((batch, SEQ, MOBA_WIDTH), BF16),
        scratch_shapes=[pltpu.VMEM((MOBA_BLOCK, SEQ), F32)],
        compiler_params=_params(("arbitrary", "arbitrary")),
        name="moba_attention",
    )(slopes, z3, z3, z3)


def _shift_rows(x, k):
    rolled = pltpu.roll(x, k, 0)
    row = lax.broadcasted_iota(jnp.int32, x.shape, 0)
    return jnp.where(row < k, 0.0, rolled)


def _gconv_kernel(bx_ref, bg_ref, cg_ref, w_ref, o_ref):
    xin = cg_ref[0].astype(F32) * bx_ref[0].astype(F32)
    w = w_ref[...]
    y = w[2:3] * xin + w[1:2] * _shift_rows(xin, 1) + w[0:1] * _shift_rows(xin, 2)
    o_ref[0] = (bg_ref[0].astype(F32) * y).astype(o_ref.dtype)


def gated_conv(z3, conv_w):
    tc = 256
    batch = z3.shape[0]
    cx, cg_, cc = OFF_BX // tc, OFF_BG // tc, OFF_CG // tc
    return pl.pallas_call(
        _gconv_kernel,
        grid=(batch, CONV_WIDTH // tc),
        in_specs=[
            pl.BlockSpec((1, SEQ, tc), lambda b, j: (b, 0, cx + j)),
            pl.BlockSpec((1, SEQ, tc), lambda b, j: (b, 0, cg_ + j)),
            pl.BlockSpec((1, SEQ, tc), lambda b, j: (b, 0, cc + j)),
            pl.BlockSpec((3, tc), lambda b, j: (0, j)),
        ],
        out_specs=pl.BlockSpec((1, SEQ, tc), lambda b, j: (b, 0, j)),
        out_shape=jax.ShapeDtypeStruct((batch, SEQ, CONV_WIDTH), BF16),
        compiler_params=_params(("arbitrary", "arbitrary")),
        name="gated_conv",
    )(z3, z3, z3, conv_w)


def _gla_kernel(q_ref, k_ref, v_ref, r_ref, ac_ref, a2_ref, ab_ref, ng_ref, o_ref, st_ref):
    cl = GLA_CHUNK
    st_ref[...] = jnp.zeros_like(st_ref)

    lane = lax.broadcasted_iota(jnp.int32, (1, 2 * GLA_DK), 1)
    head0 = jnp.where(lane < GLA_DK, 1.0, 0.0)
    head1 = 1.0 - head0
    ri = lax.broadcasted_iota(jnp.int32, (cl, cl), 0)
    ci = lax.broadcasted_iota(jnp.int32, (cl, cl), 1)
    causal = ci <= ri
    tril = jnp.where(causal, 1.0, 0.0).astype(BF16)
    sr = lax.broadcasted_iota(jnp.int32, (2 * GLA_DV, 2 * GLA_DK), 0)
    sc = lax.broadcasted_iota(jnp.int32, (2 * GLA_DV, 2 * GLA_DK), 1)
    same_head = jnp.where((sr < GLA_DV) == (sc < GLA_DK), 1.0, 0.0)

    a2h, a2l = _split_bf16(a2_ref[...], 2)
    a_bias = ab_ref[...]
    ng = ng_ref[...]

    def body(c, carry):
        rows = pl.ds(pl.multiple_of(c * cl, cl), cl)
        ah, al = _split_bf16(ac_ref[0, rows, :], 2)
        zz = _dot(ah, a2h) + _dot(ah, a2l) + _dot(al, a2h) + a_bias
        g = (jnp.minimum(zz, 0.0) - jnp.log1p(jnp.exp(-jnp.abs(zz)))) * (1.0 / GLA_TAU)
        g0, g1, g2 = _split_bf16(g, 3)
        bcum = _dot(tril, g0) + _dot(tril, g1) + _dot(tril, g2)
        b_last = bcum[cl - 1:cl, :]
        qf = q_ref[0, rows, :].astype(F32) * (GLA_DK ** -0.5)
        kf = k_ref[0, rows, :].astype(F32)
        q_i = qf * jnp.exp(bcum)
        k_i = (kf * jnp.exp(-bcum)).astype(BF16)
        k_dec = (kf * jnp.exp(b_last - bcum)).astype(BF16)
        v = v_ref[0, rows, :]

        halves = []
        for hh, hm in enumerate((head0, head1)):
            att = _dot_nt((q_i * hm).astype(BF16), k_i)
            att = jnp.where(causal, att, 0.0).astype(BF16)
            halves.append(_dot(att, v[:, hh * GLA_DV:(hh + 1) * GLA_DV]))
        st = st_ref[...]
        o = jnp.concatenate(halves, axis=1) + _dot_nt(q_i.astype(BF16), st.astype(BF16))
        st_ref[...] = st * jnp.exp(b_last) + _dot_tn(v, k_dec) * same_head

        outs = []
        for hh in range(2):
            oh = o[:, hh * GLA_DV:(hh + 1) * GLA_DV]
            ms = jnp.mean(oh * oh, axis=-1, keepdims=True)
            outs.append(oh * lax.rsqrt(ms + NORM_EPS) * ng)
        y = jnp.concatenate(outs, axis=1)
        o_ref[0, rows, :] = (_silu(r_ref[0, rows, :].astype(F32)) * y).astype(o_ref.dtype)
        return carry

    lax.fori_loop(0, SEQ // cl, body, 0)


def gla_mixer(z3, ac3, a2_pad, a_bias, norm_g):
    batch = z3.shape[0]
    kw, vw = 2 * GLA_DK, 2 * GLA_DV
    cq, ck, cv, cr = OFF_QC // kw, OFF_KC // kw, OFF_VC // vw, OFF_RC // vw
    return pl.pallas_call(
        _gla_kernel,
        grid=(batch, GLA_HEADS // 2),
        in_specs=[
            pl.BlockSpec((1, SEQ, kw), lambda b, j: (b, 0, cq + j)),
            pl.BlockSpec((1, SEQ, kw), lambda b, j: (b, 0, ck + j)),
            pl.BlockSpec((1, SEQ, vw), lambda b, j: (b, 0, cv + j)),
            pl.BlockSpec((1, SEQ, vw), lambda b, j: (b, 0, cr + j)),
            pl.BlockSpec((1, SEQ, LANES), lambda b, j: (b, 0, 0)),
            pl.BlockSpec((LANES, kw), lambda b, j: (0, j)),
            pl.BlockSpec((1, kw), lambda b, j: (0, j)),
            pl.BlockSpec((1, GLA_DV), lambda b, j: (0, 0)),
        ],
        out_specs=pl.BlockSpec((1, SEQ, vw), lambda b, j: (b, 0, j)),
        out_shape=jax.ShapeDtypeStruct((batch, SEQ, GLA_VW), BF16),
        scratch_shapes=[pltpu.VMEM((vw, kw), F32)],
        compiler_params=_params(("arbitrary", "arbitrary")),
        name="gla_mixer",
    )(z3, z3, z3, z3, ac3, a2_pad, a_bias.reshape(1, GLA_KW), norm_g.reshape(1, GLA_DV))


def _merge_kernel(h_ref, ya_ref, yb_ref, yc_ref, wg0_ref, wg1_ref, wg2_ref,
                  wa_ref, wb_ref, wc_ref, o_ref):
    h = h_ref[...]
    acc = jax.nn.sigmoid(_dot(h, wg0_ref[...])) * _dot(ya_ref[...], wa_ref[...].astype(BF16))
    acc += jax.nn.sigmoid(_dot(h, wg1_ref[...])) * _dot(yb_ref[...], wb_ref[...].astype(BF16))
    acc += jax.nn.sigmoid(_dot(h, wg2_ref[...])) * _dot(yc_ref[...], wc_ref[...].astype(BF16))
    o_ref[...] = acc.astype(o_ref.dtype)


def merge_branches(h2, ya, yb, yc, w_gl, wa, wb, wc, layer):
    tm, tn = 1024, 256
    n_tok = h2.shape[0]
    nj = D_MODEL // tn
    row = lambda i, j: (i, 0)
    colw = lambda i, j: (layer, 0, j)
    return pl.pallas_call(
        _merge_kernel,
        grid=(n_tok // tm, nj),
        in_specs=[
            pl.BlockSpec((tm, D_MODEL), row, pipeline_mode=RESIDENT),
            pl.BlockSpec((tm, MOBA_WIDTH), row, pipeline_mode=RESIDENT),
            pl.BlockSpec((tm, CONV_WIDTH), row, pipeline_mode=RESIDENT),
            pl.BlockSpec((tm, GLA_VW), row, pipeline_mode=RESIDENT),
            pl.BlockSpec((D_MODEL, tn), lambda i, j: (0, j)),
            pl.BlockSpec((D_MODEL, tn), lambda i, j: (0, nj + j)),
            pl.BlockSpec((D_MODEL, tn), lambda i, j: (0, 2 * nj + j)),
            pl.BlockSpec((None, MOBA_WIDTH, tn), colw),
            pl.BlockSpec((None, CONV_WIDTH, tn), colw),
            pl.BlockSpec((None, GLA_VW, tn), colw),
        ],
        out_specs=pl.BlockSpec((tm, tn), lambda i, j: (i, j)),
        out_shape=jax.ShapeDtypeStruct((n_tok, D_MODEL), BF16),
        compiler_params=_params(("arbitrary", "arbitrary")),
        name="merge_branches",
    )(h2, ya, yb, yc, w_gl, w_gl, w_gl, wa, wb, wc)


def _proj_residual_kernel(a_ref, w_ref, x_ref, gt_ref, o_ref):
    o_ref[...] = x_ref[...] + gt_ref[0] * _dot(a_ref[...], w_ref[...].astype(BF16))


def proj_residual(a, w, layer, x2, gt, tm, tn, name):
    m, k = a.shape
    n = w.shape[-1]
    per_batch = SEQ // tm
    return pl.pallas_call(
        _proj_residual_kernel,
        grid=(m // tm, n // tn),
        in_specs=[
            pl.BlockSpec((tm, k), lambda i, j: (i, 0), pipeline_mode=RESIDENT),
            pl.BlockSpec((None, k, tn), lambda i, j: (layer, 0, j)),
            pl.BlockSpec((tm, tn), lambda i, j: (i, j)),
            pl.BlockSpec((1, 1, tn), lambda i, j: (i // per_batch, 0, j)),
        ],
        out_specs=pl.BlockSpec((tm, tn), lambda i, j: (i, j)),
        out_shape=jax.ShapeDtypeStruct((m, n), F32),
        compiler_params=_params(("arbitrary", "arbitrary")),
        name=name,
    )(a, w, x2, gt)


def _ffn_up_kernel(a_ref, wg_ref, wu_ref, cw_ref, o_ref, wgb_ref, wub_ref, carry_ref,
                   *, tiles_per_seq, row_chunk):
    i = pl.program_id(1)

    @pl.when(i == 0)
    def _():
        wgb_ref[...] = wg_ref[...].astype(BF16)
        wub_ref[...] = wu_ref[...].astype(BF16)

    @pl.when(i % tiles_per_seq == 0)
    def _():
        carry_ref[...] = jnp.zeros(carry_ref.shape, F32)

    wg = wgb_ref[...]
    wu = wub_ref[...]
    cw = cw_ref[...]
    prev = carry_ref[...]
    row = lax.broadcasted_iota(jnp.int32, (row_chunk, wg.shape[1]), 0)
    for r in range(a_ref.shape[0] // row_chunk):
        rows = slice(r * row_chunk, (r + 1) * row_chunk)
        a = a_ref[rows, :]
        g = _dot(a, wg)
        up = _dot(a, wu)
        s1 = jnp.where(row == 0, prev[7:8], pltpu.roll(g, 1, 0))
        s2 = jnp.where(row == 0, prev[6:7], jnp.where(row == 1, prev[7:8], pltpu.roll(g, 2, 0)))
        u = cw[2:3] * g + cw[1:2] * s1 + cw[0:1] * s2
        o_ref[rows, :] = (_silu(u) * up).astype(o_ref.dtype)
        prev = g[row_chunk - 8:row_chunk, :]
    carry_ref[...] = prev


def ffn_up(h2, wg, wu, conv_w, layer):
    tm, tn = 1024, 256
    n_tok = h2.shape[0]
    return pl.pallas_call(
        functools.partial(_ffn_up_kernel, tiles_per_seq=SEQ // tm, row_chunk=256),
        grid=(D_FF // tn, n_tok // tm),
        in_specs=[
            pl.BlockSpec((tm, D_MODEL), lambda j, i: (i, 0)),
            pl.BlockSpec((None, D_MODEL, tn), lambda j, i: (layer, 0, j)),
            pl.BlockSpec((None, D_MODEL, tn), lambda j, i: (layer, 0, j)),
            pl.BlockSpec((None, 3, tn), lambda j, i: (layer, 0, j)),
        ],
        out_specs=pl.BlockSpec((tm, tn), lambda j, i: (i, j)),
        out_shape=jax.ShapeDtypeStruct((n_tok, D_FF), BF16),
        scratch_shapes=[
            pltpu.VMEM((D_MODEL, tn), BF16),
            pltpu.VMEM((D_MODEL, tn), BF16),
            pltpu.VMEM((8, tn), F32),
        ],
        compiler_params=_params(("arbitrary", "arbitrary")),
        name="ffn_up",
    )(h2, wg, wu, conv_w)


def kernel(x, c, norm1_g, w_ada, b_ada, w_in, conv_w, gla_a2, gla_a_bias, gla_norm_g,
           w_branch_a, w_branch_b, w_branch_c, w_out, norm2_g, w_ffn_gate, w_ffn_up,
           ffn_conv_w, w_ffn_down, final_norm_g):
    batch = x.shape[0]
    n_tok = batch * SEQ
    slopes = alibi_slopes(MOBA_HEADS)
    c_pad = jnp.zeros((8, D_MODEL), F32).at[:batch].set(c)
    mod = ada_modulation(c_pad, w_ada, b_ada)
    mod = mod[:, :batch].reshape(DEPTH, batch, 6, 1, D_MODEL)

    for l in range(DEPTH):
        sh1, sc1, gt1, sh2, sc2, gt2 = (mod[l, :, t] for t in range(6))

        w_gl = w_in[l, :, OFF_GATE:].astype(BF16)
        a2_pad = jnp.zeros((LANES, GLA_KW), F32).at[:GLA_RANK].set(gla_a2[l])

        h = norm_modulate(x, norm1_g[l], sc1, sh1)
        h2 = h.reshape(n_tok, D_MODEL)
        z = in_proj(h2, w_in, l, 0, Z_WIDTH, BF16, 2048, 512, "in_proj")
        ac = in_proj(h2, w_in, l, OFF_AC, LANES, F32, 2048, LANES, "in_proj_rank")
        z3 = z.reshape(batch, SEQ, Z_WIDTH)

        ya = moba_attention(z3, slopes)
        yb = gated_conv(z3, conv_w[l])
        yc = gla_mixer(z3, ac.reshape(batch, SEQ, LANES), a2_pad, gla_a_bias[l], gla_norm_g[l])

        merged = merge_branches(
            h2, ya.reshape(n_tok, MOBA_WIDTH), yb.reshape(n_tok, CONV_WIDTH),
            yc.reshape(n_tok, GLA_VW), w_gl, w_branch_a, w_branch_b, w_branch_c, l)
        x2 = proj_residual(merged, w_out, l, x.reshape(n_tok, D_MODEL), gt1, 1024, 512, "out_proj")
        x = x2.reshape(batch, SEQ, D_MODEL)

        h = norm_modulate(x, norm2_g[l], sc2, sh2)
        act = ffn_up(h.reshape(n_tok, D_MODEL), w_ffn_gate, w_ffn_up, ffn_conv_w, l)
        x2 = proj_residual(act, w_ffn_down, l, x2, gt2, 1024, 256, "ffn_down")
        x = x2.reshape(batch, SEQ, D_MODEL)

    return final_norm(x, final_norm_g)
```

```python
import functools
import math

import numpy as np
import jax
import jax.numpy as jnp
from jax import lax
from jax.experimental import pallas as pl
from jax.experimental.pallas import tpu as pltpu

F32 = jnp.float32
BF16 = jnp.bfloat16

D_MODEL = 4096
SEQ = 2048
DEPTH = 2

HEAD_DIM = 128
MOBA_HEADS = 12
MOBA_WIDTH = MOBA_HEADS * HEAD_DIM
MOBA_BLOCK = 256
MOBA_TOPK = 3
N_MOBA_BLOCKS = SEQ // MOBA_BLOCK
CONV_WIDTH = 1024
GLA_HEADS = 12
GLA_DK = 64
GLA_DV = 128
GLA_KW = GLA_HEADS * GLA_DK
GLA_VW = GLA_HEADS * GLA_DV
GLA_RANK = 16
GLA_TAU = 16.0
GLA_CHUNK = 64
D_FF = 11008
NORM_EPS = 1e-6

OFF_QA = 0
OFF_KA = OFF_QA + MOBA_WIDTH
OFF_VA = OFF_KA + MOBA_WIDTH
OFF_BX = OFF_VA + MOBA_WIDTH
OFF_BG = OFF_BX + CONV_WIDTH
OFF_CG = OFF_BG + CONV_WIDTH
OFF_QC = OFF_CG + CONV_WIDTH
OFF_KC = OFF_QC + GLA_KW
OFF_VC = OFF_KC + GLA_KW
OFF_RC = OFF_VC + GLA_VW
OFF_AC = OFF_RC + GLA_VW
OFF_GATE = OFF_AC + GLA_RANK
Z_WIDTH = OFF_AC

LANES = 128
VMEM_LIMIT = 56 * 1024 * 1024
RESIDENT = pl.Buffered(1)


def _params(sem):
    return pltpu.CompilerParams(dimension_semantics=sem, vmem_limit_bytes=VMEM_LIMIT)


def _dot(a, b):
    return jnp.dot(a, b, preferred_element_type=F32)


def _dot_nt(a, b):
    return lax.dot_general(a, b, (((1,), (1,)), ((), ())), preferred_element_type=F32)


def _dot_tn(a, b):
    return lax.dot_general(a, b, (((0,), (0,)), ((), ())), preferred_element_type=F32)


def _split_bf16(x, parts):
    out = []
    r = x
    for _ in range(parts):
        p = r.astype(BF16)
        out.append(p)
        r = r - p.astype(F32)
    return out


def _silu(x):
    return x * jax.nn.sigmoid(x)


def alibi_slopes(n):
    def pow2(m):
        start = 2.0 ** (-(2.0 ** -(math.log2(m) - 3)))
        return [start ** (i + 1) for i in range(m)]
    if math.log2(n).is_integer():
        s = pow2(n)
    else:
        closest = 2 ** math.floor(math.log2(n))
        s = pow2(closest) + pow2(2 * closest)[0::2][: n - closest]
    return jnp.asarray(np.array(s, dtype=np.float32))


def _ada_kernel(c_ref, w_ref, b_ref, o_ref):
    ca = _silu(c_ref[...]).astype(BF16)
    o_ref[0] = _dot(ca, w_ref[0].astype(BF16)) + b_ref[0]


def ada_modulation(c_pad, w_ada, b_ada):
    n_out = w_ada.shape[-1]
    tn = 512
    return pl.pallas_call(
        _ada_kernel,
        grid=(DEPTH, n_out // tn),
        in_specs=[
            pl.BlockSpec((8, D_MODEL), lambda l, j: (0, 0)),
            pl.BlockSpec((1, D_MODEL, tn), lambda l, j: (l, 0, j)),
            pl.BlockSpec((1, 1, tn), lambda l, j: (l, 0, j)),
        ],
        out_specs=pl.BlockSpec((1, 8, tn), lambda l, j: (l, 0, j)),
        out_shape=jax.ShapeDtypeStruct((DEPTH, 8, n_out), F32),
        compiler_params=_params(("arbitrary", "arbitrary")),
        name="ada_modulation",
    )(c_pad, w_ada, b_ada.reshape(DEPTH, 1, n_out))


def _norm_mod_kernel(x_ref, g_ref, sc_ref, sh_ref, o_ref):
    x = x_ref[0]
    y = x * lax.rsqrt(jnp.mean(x * x, axis=-1, keepdims=True) + NORM_EPS) * g_ref[...]
    o_ref[0] = (y * (1.0 + sc_ref[0]) + sh_ref[0]).astype(o_ref.dtype)


def _norm_kernel(x_ref, g_ref, o_ref):
    x = x_ref[0]
    y = x * lax.rsqrt(jnp.mean(x * x, axis=-1, keepdims=True) + NORM_EPS) * g_ref[...]
    o_ref[0] = y.astype(o_ref.dtype)


def norm_modulate(x, g, sc, sh):
    tm = 256
    batch = x.shape[0]
    return pl.pallas_call(
        _norm_mod_kernel,
        grid=(batch, SEQ // tm),
        in_specs=[
            pl.BlockSpec((1, tm, D_MODEL), lambda b, i: (b, i, 0)),
            pl.BlockSpec((1, D_MODEL), lambda b, i: (0, 0)),
            pl.BlockSpec((1, 1, D_MODEL), lambda b, i: (b, 0, 0)),
            pl.BlockSpec((1, 1, D_MODEL), lambda b, i: (b, 0, 0)),
        ],
        out_specs=pl.BlockSpec((1, tm, D_MODEL), lambda b, i: (b, i, 0)),
        out_shape=jax.ShapeDtypeStruct((batch, SEQ, D_MODEL), BF16),
        compiler_params=_params(("arbitrary", "arbitrary")),
        name="norm_modulate",
    )(x, g.reshape(1, D_MODEL), sc, sh)


def final_norm(x, g):
    tm = 256
    batch = x.shape[0]
    return pl.pallas_call(
        _norm_kernel,
        grid=(batch, SEQ // tm),
        in_specs=[
            pl.BlockSpec((1, tm, D_MODEL), lambda b, i: (b, i, 0)),
            pl.BlockSpec((1, D_MODEL), lambda b, i: (0, 0)),
        ],
        out_specs=pl.BlockSpec((1, tm, D_MODEL), lambda b, i: (b, i, 0)),
        out_shape=jax.ShapeDtypeStruct((batch, SEQ, D_MODEL), F32),
        compiler_params=_params(("arbitrary", "arbitrary")),
        name="final_norm",
    )(x, g.reshape(1, D_MODEL))


def _mm_kernel(a_ref, w_ref, o_ref):
    o_ref[...] = _dot(a_ref[...], w_ref[...].astype(BF16)).astype(o_ref.dtype)


def in_proj(a, w_in, layer, col0, n, out_dtype, tm, tn, name):
    m, k = a.shape
    j0 = col0 // tn
    return pl.pallas_call(
        _mm_kernel,
        grid=(m // tm, n // tn),
        in_specs=[
            pl.BlockSpec((tm, k), lambda i, j: (i, 0), pipeline_mode=RESIDENT),
            pl.BlockSpec((None, k, tn), lambda i, j: (layer, 0, j0 + j)),
        ],
        out_specs=pl.BlockSpec((tm, tn), lambda i, j: (i, j)),
        out_shape=jax.ShapeDtypeStruct((m, n), out_dtype),
        compiler_params=_params(("arbitrary", "arbitrary")),
        name=name,
    )(a, w_in)


def _realign_kernel(main_ref, tail_ref, o_ref):
    tn = main_ref.shape[1]
    full = jnp.concatenate([main_ref[...], tail_ref[...]], axis=1)
    o_ref[...] = pltpu.roll(full, tn + LANES - GLA_RANK, 1)[:, :tn].astype(o_ref.dtype)


def gate_weights(w_in, layer):
    tr, tn = 512, 1024
    n = 3 * D_MODEL
    j0 = OFF_AC // tn
    return pl.pallas_call(
        _realign_kernel,
        grid=(D_MODEL // tr, n // tn),
        in_specs=[
            pl.BlockSpec((None, tr, tn), lambda r, j: (layer, r, j0 + j)),
            pl.BlockSpec((None, tr, LANES), lambda r, j: (layer, r, (OFF_AC + (j + 1) * tn) // LANES)),
        ],
        out_specs=pl.BlockSpec((tr, tn), lambda r, j: (r, j)),
        out_shape=jax.ShapeDtypeStruct((D_MODEL, n), BF16),
        compiler_params=_params(("arbitrary", "arbitrary")),
        name="gate_weights",
    )(w_in, w_in)


def _moba_kernel(slopes_ref, q_ref, k_ref, v_ref, o_ref, s_ref):
    slope = slopes_ref[pl.program_id(1)]
    scale = HEAD_DIM ** -0.5
    blk = MOBA_BLOCK

    r = lax.broadcasted_iota(jnp.int32, (LANES, SEQ), 0)
    t = lax.broadcasted_iota(jnp.int32, (LANES, SEQ), 1)
    pm = jnp.where(lax.shift_right_logical(t, 8) == r, 1.0 / blk, 0.0).astype(BF16)
    kmh, kml = _split_bf16(_dot(pm, k_ref[0]), 2)

    row = lax.broadcasted_iota(jnp.int32, (blk, blk), 0)
    col = lax.broadcasted_iota(jnp.int32, (blk, blk), 1)
    srel = slope * (col - row).astype(F32)
    causal = col <= row
    eye = jnp.where(row == col, 1.0, 0.0).astype(BF16)
    blk_id = lax.broadcasted_iota(jnp.int32, (N_MOBA_BLOCKS, blk), 0)

    for qi in range(N_MOBA_BLOCKS):
        rows = slice(qi * blk, (qi + 1) * blk)
        q_raw = q_ref[0, rows, :]
        qs = (q_raw.astype(F32) * scale).astype(BF16)

        sel_b = None
        if qi > MOBA_TOPK:
            g = (_dot_nt(kmh, q_raw) + _dot_nt(kml, q_raw))[:N_MOBA_BLOCKS] * scale
            rank = jnp.zeros_like(g)
            for m in range(qi):
                gm = g[m:m + 1, :]
                tie = jnp.where(blk_id > m, 1.0, 0.0)
                rank = rank + jnp.where(gm > g, 1.0, jnp.where(gm == g, tie, 0.0))
            sel = jnp.where(rank < float(MOBA_TOPK), 1.0, 0.0)
            sel_b = [_dot_nt(eye, jnp.broadcast_to(sel[n:n + 1, :], (LANES, blk)).astype(BF16))
                     for n in range(qi)]

        m_run = None
        for n in range(qi + 1):
            cols = slice(n * blk, (n + 1) * blk)
            s = _dot_nt(qs, k_ref[0, cols, :]) + srel
            if n == qi:
                s = jnp.where(causal, s, -jnp.inf)
            s_ref[:, cols] = s
            rm = jnp.max(s, axis=1, keepdims=True) - slope * float(blk * (qi - n))
            if sel_b is not None and n < qi:
                rm = jnp.where(sel_b[n][:, 0:1] > 0.5, rm, -jnp.inf)
            m_run = rm if m_run is None else jnp.maximum(m_run, rm)

        l_run = None
        acc = None
        for n in range(qi + 1):
            cols = slice(n * blk, (n + 1) * blk)
            p = jnp.exp(s_ref[:, cols] - (m_run + slope * float(blk * (qi - n))))
            rs = jnp.sum(p, axis=1, keepdims=True)
            pv = _dot(p.astype(BF16), v_ref[0, cols, :])
            if sel_b is not None and n < qi:
                rs = jnp.where(sel_b[n][:, 0:1] > 0.5, rs, 0.0)
                pv = jnp.where(sel_b[n] > 0.5, pv, 0.0)
            l_run = rs if l_run is None else l_run + rs
            acc = pv if acc is None else acc + pv
        o_ref[0, rows, :] = (acc / l_run).astype(o_ref.dtype)


def moba_attention(z3, slopes):
    batch = z3.shape[0]
    cq, ck, cv = OFF_QA // HEAD_DIM, OFF_KA // HEAD_DIM, OFF_VA // HEAD_DIM
    return pl.pallas_call(
        _moba_kernel,
        grid=(batch, MOBA_HEADS),
        in_specs=[
            pl.BlockSpec(memory_space=pltpu.SMEM),
            pl.BlockSpec((1, SEQ, HEAD_DIM), lambda b, h: (b, 0, cq + h)),
            pl.BlockSpec((1, SEQ, HEAD_DIM), lambda b, h: (b, 0, ck + h)),
            pl.BlockSpec((1, SEQ, HEAD_DIM), lambda b, h: (b, 0, cv + h)),
        ],
        out_specs=pl.BlockSpec((1, SEQ, HEAD_DIM), lambda b, h: (b, 0, h)),
        out_shape=jax.ShapeDtypeStruct((batch, SEQ, MOBA_WIDTH), BF16),
        scratch_shapes=[pltpu.VMEM((MOBA_BLOCK, SEQ), F32)],
        compiler_params=_params(("arbitrary", "arbitrary")),
        name="moba_attention",
    )(slopes, z3, z3, z3)


def _shift_rows(x, k):
    rolled = pltpu.roll(x, k, 0)
    row = lax.broadcasted_iota(jnp.int32, x.shape, 0)
    return jnp.where(row < k, 0.0, rolled)


def _gconv_kernel(bx_ref, bg_ref, cg_ref, w_ref, o_ref):
    xin = cg_ref[0].astype(F32) * bx_ref[0].astype(F32)
    w = w_ref[...]
    y = w[2:3] * xin + w[1:2] * _shift_rows(xin, 1) + w[0:1] * _shift_rows(xin, 2)
    o_ref[0] = (bg_ref[0].astype(F32) * y).astype(o_ref.dtype)


def gated_conv(z3, conv_w):
    tc = 256
    batch = z3.shape[0]
    cx, cg_, cc = OFF_BX // tc, OFF_BG // tc, OFF_CG // tc
    return pl.pallas_call(
        _gconv_kernel,
        grid=(batch, CONV_WIDTH // tc),
        in_specs=[
            pl.BlockSpec((1, SEQ, tc), lambda b, j: (b, 0, cx + j)),
            pl.BlockSpec((1, SEQ, tc), lambda b, j: (b, 0, cg_ + j)),
            pl.BlockSpec((1, SEQ, tc), lambda b, j: (b, 0, cc + j)),
            pl.BlockSpec((3, tc), lambda b, j: (0, j)),
        ],
        out_specs=pl.BlockSpec((1, SEQ, tc), lambda b, j: (b, 0, j)),
        out_shape=jax.ShapeDtypeStruct((batch, SEQ, CONV_WIDTH), BF16),
        compiler_params=_params(("arbitrary", "arbitrary")),
        name="gated_conv",
    )(z3, z3, z3, conv_w)


GLA_GROUP = 256


def _gla_kernel(q_ref, k_ref, v_ref, r_ref, ac_ref, a2_ref, ab_ref, ng_ref, o_ref):
    cl, gr = GLA_CHUNK, GLA_GROUP

    lane = lax.broadcasted_iota(jnp.int32, (1, 2 * GLA_DK), 1)
    head0 = jnp.where(lane < GLA_DK, 1.0, 0.0)
    head1 = 1.0 - head0
    ri = lax.broadcasted_iota(jnp.int32, (gr, gr), 0)
    ci = lax.broadcasted_iota(jnp.int32, (gr, gr), 1)
    same_chunk = lax.shift_right_logical(ri, 6) == lax.shift_right_logical(ci, 6)
    causal = jnp.logical_and(same_chunk, ci <= ri)
    tril = jnp.where(causal, 1.0, 0.0).astype(BF16)
    ones = jnp.where(same_chunk, 1.0, 0.0).astype(BF16)
    sr = lax.broadcasted_iota(jnp.int32, (2 * GLA_DV, 2 * GLA_DK), 0)
    sc = lax.broadcasted_iota(jnp.int32, (2 * GLA_DV, 2 * GLA_DK), 1)
    same_head = jnp.where((sr < GLA_DV) == (sc < GLA_DK), 1.0, 0.0)

    a2h, a2l = _split_bf16(a2_ref[...], 2)
    a_bias = ab_ref[...]
    ng = ng_ref[...]

    st = jnp.zeros((2 * GLA_DV, 2 * GLA_DK), F32)
    for grp in range(SEQ // gr):
        rows = slice(grp * gr, (grp + 1) * gr)
        ah, al = _split_bf16(ac_ref[0, rows, :], 2)
        zz = _dot(ah, a2h) + _dot(ah, a2l) + _dot(al, a2h) + a_bias
        g = (jnp.minimum(zz, 0.0) - jnp.log1p(jnp.exp(-jnp.abs(zz)))) * (1.0 / GLA_TAU)
        g0, g1, g2 = _split_bf16(g, 3)
        bcum = _dot(tril, g0) + _dot(tril, g1) + _dot(tril, g2)
        btot = _dot(ones, g0) + _dot(ones, g1) + _dot(ones, g2)
        qf = q_ref[0, rows, :].astype(F32) * (GLA_DK ** -0.5)
        kf = k_ref[0, rows, :].astype(F32)
        q_i = qf * jnp.exp(bcum)
        k_i = (kf * jnp.exp(-bcum)).astype(BF16)
        k_dec = (kf * jnp.exp(btot - bcum)).astype(BF16)
        dec = jnp.exp(btot)
        v = v_ref[0, rows, :]

        halves = []
        for hh, hm in enumerate((head0, head1)):
            att = _dot_nt((q_i * hm).astype(BF16), k_i)
            att = jnp.where(causal, att, 0.0).astype(BF16)
            halves.append(_dot(att, v[:, hh * GLA_DV:(hh + 1) * GLA_DV]))
        o_intra = jnp.concatenate(halves, axis=1)

        q_b = q_i.astype(BF16)
        inter = []
        for c in range(gr // cl):
            cr = slice(c * cl, (c + 1) * cl)
            inter.append(_dot_nt(q_b[cr], st.astype(BF16)))
            st = st * dec[c * cl:c * cl + 1, :] + _dot_tn(v[cr], k_dec[cr]) * same_head
        o = o_intra + jnp.concatenate(inter, axis=0)

        outs = []
        for hh in range(2):
            oh = o[:, hh * GLA_DV:(hh + 1) * GLA_DV]
            ms = jnp.mean(oh * oh, axis=-1, keepdims=True)
            outs.append(oh * lax.rsqrt(ms + NORM_EPS) * ng)
        y = jnp.concatenate(outs, axis=1)
        o_ref[0, rows, :] = (_silu(r_ref[0, rows, :].astype(F32)) * y).astype(o_ref.dtype)


def gla_mixer(z3, ac3, a2_pad, a_bias, norm_g):
    batch = z3.shape[0]
    kw, vw = 2 * GLA_DK, 2 * GLA_DV
    cq, ck, cv, cr = OFF_QC // kw, OFF_KC // kw, OFF_VC // vw, OFF_RC // vw
    return pl.pallas_call(
        _gla_kernel,
        grid=(batch, GLA_HEADS // 2),
        in_specs=[
            pl.BlockSpec((1, SEQ, kw), lambda b, j: (b, 0, cq + j)),
            pl.BlockSpec((1, SEQ, kw), lambda b, j: (b, 0, ck + j)),
            pl.BlockSpec((1, SEQ, vw), lambda b, j: (b, 0, cv + j)),
            pl.BlockSpec((1, SEQ, vw), lambda b, j: (b, 0, cr + j)),
            pl.BlockSpec((1, SEQ, LANES), lambda b, j: (b, 0, 0)),
            pl.BlockSpec((LANES, kw), lambda b, j: (0, j)),
            pl.BlockSpec((1, kw), lambda b, j: (0, j)),
            pl.BlockSpec((1, GLA_DV), lambda b, j: (0, 0)),
        ],
        out_specs=pl.BlockSpec((1, SEQ, vw), lambda b, j: (b, 0, j)),
        out_shape=jax.ShapeDtypeStruct((batch, SEQ, GLA_VW), BF16),
        compiler_params=_params(("arbitrary", "arbitrary")),
        name="gla_mixer",
    )(z3, z3, z3, z3, ac3, a2_pad, a_bias.reshape(1, GLA_KW), norm_g.reshape(1, GLA_DV))


def _merge_kernel(h_ref, ya_ref, yb_ref, yc_ref, wg0_ref, wg1_ref, wg2_ref,
                  wa_ref, wb_ref, wc_ref, o_ref):
    h = h_ref[...]
    acc = jax.nn.sigmoid(_dot(h, wg0_ref[...])) * _dot(ya_ref[...], wa_ref[...].astype(BF16))
    acc += jax.nn.sigmoid(_dot(h, wg1_ref[...])) * _dot(yb_ref[...], wb_ref[...].astype(BF16))
    acc += jax.nn.sigmoid(_dot(h, wg2_ref[...])) * _dot(yc_ref[...], wc_ref[...].astype(BF16))
    o_ref[...] = acc.astype(o_ref.dtype)


def merge_branches(h2, ya, yb, yc, w_gl, wa, wb, wc, layer):
    tm, tn = 1024, 256
    n_tok = h2.shape[0]
    nj = D_MODEL // tn
    row = lambda i, j: (i, 0)
    colw = lambda i, j: (layer, 0, j)
    return pl.pallas_call(
        _merge_kernel,
        grid=(n_tok // tm, nj),
        in_specs=[
            pl.BlockSpec((tm, D_MODEL), row, pipeline_mode=RESIDENT),
            pl.BlockSpec((tm, MOBA_WIDTH), row, pipeline_mode=RESIDENT),
            pl.BlockSpec((tm, CONV_WIDTH), row, pipeline_mode=RESIDENT),
            pl.BlockSpec((tm, GLA_VW), row, pipeline_mode=RESIDENT),
            pl.BlockSpec((D_MODEL, tn), lambda i, j: (0, j)),
            pl.BlockSpec((D_MODEL, tn), lambda i, j: (0, nj + j)),
            pl.BlockSpec((D_MODEL, tn), lambda i, j: (0, 2 * nj + j)),
            pl.BlockSpec((None, MOBA_WIDTH, tn), colw),
            pl.BlockSpec((None, CONV_WIDTH, tn), colw),
            pl.BlockSpec((None, GLA_VW, tn), colw),
        ],
        out_specs=pl.BlockSpec((tm, tn), lambda i, j: (i, j)),
        out_shape=jax.ShapeDtypeStruct((n_tok, D_MODEL), BF16),
        compiler_params=_params(("arbitrary", "arbitrary")),
        name="merge_branches",
    )(h2, ya, yb, yc, w_gl, w_gl, w_gl, wa, wb, wc)


def _proj_residual_kernel(a_ref, w_ref, x_ref, gt_ref, o_ref):
    o_ref[...] = x_ref[...] + gt_ref[0] * _dot(a_ref[...], w_ref[...].astype(BF16))


def proj_residual(a, w, layer, x2, gt, tm, tn, name):
    m, k = a.shape
    n = w.shape[-1]
    per_batch = SEQ // tm
    return pl.pallas_call(
        _proj_residual_kernel,
        grid=(m // tm, n // tn),
        in_specs=[
            pl.BlockSpec((tm, k), lambda i, j: (i, 0), pipeline_mode=RESIDENT),
            pl.BlockSpec((None, k, tn), lambda i, j: (layer, 0, j)),
            pl.BlockSpec((tm, tn), lambda i, j: (i, j)),
            pl.BlockSpec((1, 1, tn), lambda i, j: (i // per_batch, 0, j)),
        ],
        out_specs=pl.BlockSpec((tm, tn), lambda i, j: (i, j)),
        out_shape=jax.ShapeDtypeStruct((m, n), F32),
        compiler_params=_params(("arbitrary", "arbitrary")),
        name=name,
    )(a, w, x2, gt)


def _ffn_up_kernel(a_ref, wg_ref, wu_ref, cw_ref, o_ref, *, row_chunk):
    wg = wg_ref[...].astype(BF16)
    wu = wu_ref[...].astype(BF16)
    cw = cw_ref[...]
    prev = jnp.zeros((8, wg.shape[1]), F32)
    row = lax.broadcasted_iota(jnp.int32, (row_chunk, wg.shape[1]), 0)
    for r in range(a_ref.shape[0] // row_chunk):
        rows = slice(r * row_chunk, (r + 1) * row_chunk)
        a = a_ref[rows, :]
        g = _dot(a, wg)
        up = _dot(a, wu)
        s1 = jnp.where(row == 0, prev[7:8], pltpu.roll(g, 1, 0))
        s2 = jnp.where(row == 0, prev[6:7], jnp.where(row == 1, prev[7:8], pltpu.roll(g, 2, 0)))
        u = cw[2:3] * g + cw[1:2] * s1 + cw[0:1] * s2
        o_ref[rows, :] = (_silu(u) * up).astype(o_ref.dtype)
        prev = g[row_chunk - 8:row_chunk, :]


def ffn_up(h2, wg, wu, conv_w, layer):
    tm, tn = SEQ, 256
    n_tok = h2.shape[0]
    return pl.pallas_call(
        functools.partial(_ffn_up_kernel, row_chunk=512),
        grid=(n_tok // tm, D_FF // tn),
        in_specs=[
            pl.BlockSpec((tm, D_MODEL), lambda i, j: (i, 0), pipeline_mode=RESIDENT),
            pl.BlockSpec((None, D_MODEL, tn), lambda i, j: (layer, 0, j)),
            pl.BlockSpec((None, D_MODEL, tn), lambda i, j: (layer, 0, j)),
            pl.BlockSpec((None, 3, tn), lambda i, j: (layer, 0, j)),
        ],
        out_specs=pl.BlockSpec((tm, tn), lambda i, j: (i, j)),
        out_shape=jax.ShapeDtypeStruct((n_tok, D_FF), BF16),
        compiler_params=_params(("arbitrary", "arbitrary")),
        name="ffn_up",
    )(h2, wg, wu, conv_w)


def kernel(x, c, norm1_g, w_ada, b_ada, w_in, conv_w, gla_a2, gla_a_bias, gla_norm_g,
           w_branch_a, w_branch_b, w_branch_c, w_out, norm2_g, w_ffn_gate, w_ffn_up,
           ffn_conv_w, w_ffn_down, final_norm_g):
    batch = x.shape[0]
    n_tok = batch * SEQ
    slopes = alibi_slopes(MOBA_HEADS)
    c_pad = jnp.zeros((8, D_MODEL), F32).at[:batch].set(c)
    mod = ada_modulation(c_pad, w_ada, b_ada)
    mod = mod[:, :batch].reshape(DEPTH, batch, 6, 1, D_MODEL)

    for l in range(DEPTH):
        sh1, sc1, gt1, sh2, sc2, gt2 = (mod[l, :, t] for t in range(6))

        w_gl = gate_weights(w_in, l)
        a2_pad = jnp.zeros((LANES, GLA_KW), F32).at[:GLA_RANK].set(gla_a2[l])

        h = norm_modulate(x, norm1_g[l], sc1, sh1)
        h2 = h.reshape(n_tok, D_MODEL)
        z = in_proj(h2, w_in, l, 0, Z_WIDTH, BF16, 2048, 512, "in_proj")
        ac = in_proj(h2, w_in, l, OFF_AC, LANES, F32, 2048, LANES, "in_proj_rank")
        z3 = z.reshape(batch, SEQ, Z_WIDTH)

        ya = moba_attention(z3, slopes)
        yb = gated_conv(z3, conv_w[l])
        yc = gla_mixer(z3, ac.reshape(batch, SEQ, LANES), a2_pad, gla_a_bias[l], gla_norm_g[l])

        merged = merge_branches(
            h2, ya.reshape(n_tok, MOBA_WIDTH), yb.reshape(n_tok, CONV_WIDTH),
            yc.reshape(n_tok, GLA_VW), w_gl, w_branch_a, w_branch_b, w_branch_c, l)
        x2 = proj_residual(merged, w_out, l, x.reshape(n_tok, D_MODEL), gt1, 2048, 512, "out_proj")
        x = x2.reshape(batch, SEQ, D_MODEL)

        h = norm_modulate(x, norm2_g[l], sc2, sh2)
        act = ffn_up(h.reshape(n_tok, D_MODEL), w_ffn_gate, w_ffn_up, ffn_conv_w, l)
        x2 = proj_residual(act, w_ffn_down, l, x2, gt2, 1024, 256, "ffn_down")
        x = x2.reshape(batch, SEQ, D_MODEL)

    return final_norm(x, final_norm_g)
```

```python
import functools
import math

import numpy as np
import jax
import jax.numpy as jnp
from jax import lax
from jax.experimental import pallas as pl
from jax.experimental.pallas import tpu as pltpu

F32 = jnp.float32
BF16 = jnp.bfloat16

D_MODEL = 4096
SEQ = 2048
DEPTH = 2

HEAD_DIM = 128
MOBA_HEADS = 12
MOBA_WIDTH = MOBA_HEADS * HEAD_DIM
MOBA_BLOCK = 256
MOBA_TOPK = 3
N_MOBA_BLOCKS = SEQ // MOBA_BLOCK
CONV_WIDTH = 1024
GLA_HEADS = 12
GLA_DK = 64
GLA_DV = 128
GLA_KW = GLA_HEADS * GLA_DK
GLA_VW = GLA_HEADS * GLA_DV
GLA_RANK = 16
GLA_TAU = 16.0
GLA_CHUNK = 64
D_FF = 11008
NORM_EPS = 1e-6

OFF_QA = 0
OFF_KA = OFF_QA + MOBA_WIDTH
OFF_VA = OFF_KA + MOBA_WIDTH
OFF_BX = OFF_VA + MOBA_WIDTH
OFF_BG = OFF_BX + CONV_WIDTH
OFF_CG = OFF_BG + CONV_WIDTH
OFF_QC = OFF_CG + CONV_WIDTH
OFF_KC = OFF_QC + GLA_KW
OFF_VC = OFF_KC + GLA_KW
OFF_RC = OFF_VC + GLA_VW
OFF_AC = OFF_RC + GLA_VW
OFF_GATE = OFF_AC + GLA_RANK
Z_WIDTH = OFF_AC

LANES = 128
VMEM_LIMIT = 56 * 1024 * 1024
RESIDENT = pl.Buffered(1)


def _params(sem):
    return pltpu.CompilerParams(dimension_semantics=sem, vmem_limit_bytes=VMEM_LIMIT)


def _dot(a, b):
    return jnp.dot(a, b, preferred_element_type=F32)


def _dot_nt(a, b):
    return lax.dot_general(a, b, (((1,), (1,)), ((), ())), preferred_element_type=F32)


def _dot_tn(a, b):
    return lax.dot_general(a, b, (((0,), (0,)), ((), ())), preferred_element_type=F32)


def _split_bf16(x, parts):
    out = []
    r = x
    for _ in range(parts):
        p = r.astype(BF16)
        out.append(p)
        r = r - p.astype(F32)
    return out


def _silu(x):
    return x * jax.nn.sigmoid(x)


def alibi_slopes(n):
    def pow2(m):
        start = 2.0 ** (-(2.0 ** -(math.log2(m) - 3)))
        return [start ** (i + 1) for i in range(m)]
    if math.log2(n).is_integer():
        s = pow2(n)
    else:
        closest = 2 ** math.floor(math.log2(n))
        s = pow2(closest) + pow2(2 * closest)[0::2][: n - closest]
    return jnp.asarray(np.array(s, dtype=np.float32))


def _ada_kernel(c_ref, w_ref, b_ref, o_ref):
    ca = _silu(c_ref[...]).astype(BF16)
    o_ref[0] = _dot(ca, w_ref[0].astype(BF16)) + b_ref[0]


def ada_modulation(c_pad, w_ada, b_ada):
    n_out = w_ada.shape[-1]
    tn = 512
    return pl.pallas_call(
        _ada_kernel,
        grid=(DEPTH, n_out // tn),
        in_specs=[
            pl.BlockSpec((8, D_MODEL), lambda l, j: (0, 0)),
            pl.BlockSpec((1, D_MODEL, tn), lambda l, j: (l, 0, j)),
            pl.BlockSpec((1, 1, tn), lambda l, j: (l, 0, j)),
        ],
        out_specs=pl.BlockSpec((1, 8, tn), lambda l, j: (l, 0, j)),
        out_shape=jax.ShapeDtypeStruct((DEPTH, 8, n_out), F32),
        compiler_params=_params(("arbitrary", "arbitrary")),
        name="ada_modulation",
    )(c_pad, w_ada, b_ada.reshape(DEPTH, 1, n_out))


def _norm_mod_kernel(x_ref, g_ref, sc_ref, sh_ref, o_ref):
    x = x_ref[0]
    y = x * lax.rsqrt(jnp.mean(x * x, axis=-1, keepdims=True) + NORM_EPS) * g_ref[...]
    o_ref[0] = (y * (1.0 + sc_ref[0]) + sh_ref[0]).astype(o_ref.dtype)


def _norm_kernel(x_ref, g_ref, o_ref):
    x = x_ref[0]
    y = x * lax.rsqrt(jnp.mean(x * x, axis=-1, keepdims=True) + NORM_EPS) * g_ref[...]
    o_ref[0] = y.astype(o_ref.dtype)


def norm_modulate(x, g, sc, sh):
    tm = 256
    batch = x.shape[0]
    return pl.pallas_call(
        _norm_mod_kernel,
        grid=(batch, SEQ // tm),
        in_specs=[
            pl.BlockSpec((1, tm, D_MODEL), lambda b, i: (b, i, 0)),
            pl.BlockSpec((1, D_MODEL), lambda b, i: (0, 0)),
            pl.BlockSpec((1, 1, D_MODEL), lambda b, i: (b, 0, 0)),
            pl.BlockSpec((1, 1, D_MODEL), lambda b, i: (b, 0, 0)),
        ],
        out_specs=pl.BlockSpec((1, tm, D_MODEL), lambda b, i: (b, i, 0)),
        out_shape=jax.ShapeDtypeStruct((batch, SEQ, D_MODEL), BF16),
        compiler_params=_params(("arbitrary", "arbitrary")),
        name="norm_modulate",
    )(x, g.reshape(1, D_MODEL), sc, sh)


def final_norm(x, g):
    tm = 256
    batch = x.shape[0]
    return pl.pallas_call(
        _norm_kernel,
        grid=(batch, SEQ // tm),
        in_specs=[
            pl.BlockSpec((1, tm, D_MODEL), lambda b, i: (b, i, 0)),
            pl.BlockSpec((1, D_MODEL), lambda b, i: (0, 0)),
        ],
        out_specs=pl.BlockSpec((1, tm, D_MODEL), lambda b, i: (b, i, 0)),
        out_shape=jax.ShapeDtypeStruct((batch, SEQ, D_MODEL), F32),
        compiler_params=_params(("arbitrary", "arbitrary")),
        name="final_norm",
    )(x, g.reshape(1, D_MODEL))


def _mm_kernel(a_ref, w_ref, o_ref):
    o_ref[...] = _dot_nt(a_ref[...], w_ref[...].astype(BF16)).astype(o_ref.dtype)


def in_proj(a, w_in_t, layer, col0, n, out_dtype, tm, tn, name):
    m, k = a.shape
    j0 = col0 // tn
    return pl.pallas_call(
        _mm_kernel,
        grid=(m // tm, n // tn),
        in_specs=[
            pl.BlockSpec((tm, k), lambda i, j: (i, 0), pipeline_mode=RESIDENT),
            pl.BlockSpec((None, tn, k), lambda i, j: (layer, j0 + j, 0)),
        ],
        out_specs=pl.BlockSpec((tm, tn), lambda i, j: (i, j)),
        out_shape=jax.ShapeDtypeStruct((m, n), out_dtype),
        compiler_params=_params(("arbitrary", "arbitrary")),
        name=name,
    )(a, w_in_t)


def _cast_kernel(w_ref, o_ref):
    o_ref[...] = w_ref[0].astype(o_ref.dtype)


def gate_weights(w_in_t, layer):
    tr = 512
    n = 3 * D_MODEL
    return pl.pallas_call(
        _cast_kernel,
        grid=(n // tr,),
        in_specs=[pl.BlockSpec((pl.Element(1), pl.Element(tr), pl.Element(D_MODEL)),
                               lambda r: (layer, pl.multiple_of(OFF_GATE + r * tr, GLA_RANK), 0))],
        out_specs=pl.BlockSpec((tr, D_MODEL), lambda r: (r, 0)),
        out_shape=jax.ShapeDtypeStruct((n, D_MODEL), BF16),
        compiler_params=_params(("arbitrary",)),
        name="gate_weights",
    )(w_in_t)


def _moba_kernel(slopes_ref, q_ref, k_ref, v_ref, o_ref, s_ref):
    slope = slopes_ref[pl.program_id(1)]
    scale = HEAD_DIM ** -0.5
    blk = MOBA_BLOCK

    r = lax.broadcasted_iota(jnp.int32, (LANES, SEQ), 0)
    t = lax.broadcasted_iota(jnp.int32, (LANES, SEQ), 1)
    pm = jnp.where(lax.shift_right_logical(t, 8) == r, 1.0 / blk, 0.0).astype(BF16)
    kmh, kml = _split_bf16(_dot(pm, k_ref[0]), 2)

    row = lax.broadcasted_iota(jnp.int32, (blk, blk), 0)
    col = lax.broadcasted_iota(jnp.int32, (blk, blk), 1)
    srel = slope * (col - row).astype(F32)
    causal = col <= row
    eye = jnp.where(row == col, 1.0, 0.0).astype(BF16)
    blk_id = lax.broadcasted_iota(jnp.int32, (N_MOBA_BLOCKS, blk), 0)

    for qi in range(N_MOBA_BLOCKS):
        rows = slice(qi * blk, (qi + 1) * blk)
        q_raw = q_ref[0, rows, :]
        qs = (q_raw.astype(F32) * scale).astype(BF16)

        sel_b = None
        if qi > MOBA_TOPK:
            g = (_dot_nt(kmh, q_raw) + _dot_nt(kml, q_raw))[:N_MOBA_BLOCKS] * scale
            rank = jnp.zeros_like(g)
            for m in range(qi):
                gm = g[m:m + 1, :]
                tie = jnp.where(blk_id > m, 1.0, 0.0)
                rank = rank + jnp.where(gm > g, 1.0, jnp.where(gm == g, tie, 0.0))
            sel = jnp.where(rank < float(MOBA_TOPK), 1.0, 0.0)
            sel_b = [_dot_nt(eye, jnp.broadcast_to(sel[n:n + 1, :], (LANES, blk)).astype(BF16))
                     for n in range(qi)]

        m_run = None
        for n in range(qi + 1):
            cols = slice(n * blk, (n + 1) * blk)
            s = _dot_nt(qs, k_ref[0, cols, :]) + srel
            if n == qi:
                s = jnp.where(causal, s, -jnp.inf)
            s_ref[:, cols] = s
            rm = jnp.max(s, axis=1, keepdims=True) - slope * float(blk * (qi - n))
            if sel_b is not None and n < qi:
                rm = jnp.where(sel_b[n][:, 0:1] > 0.5, rm, -jnp.inf)
            m_run = rm if m_run is None else jnp.maximum(m_run, rm)

        l_run = None
        acc = None
        for n in range(qi + 1):
            cols = slice(n * blk, (n + 1) * blk)
            p = jnp.exp(s_ref[:, cols] - (m_run + slope * float(blk * (qi - n))))
            rs = jnp.sum(p, axis=1, keepdims=True)
            pv = _dot(p.astype(BF16), v_ref[0, cols, :])
            if sel_b is not None and n < qi:
                rs = jnp.where(sel_b[n][:, 0:1] > 0.5, rs, 0.0)
                pv = jnp.where(sel_b[n] > 0.5, pv, 0.0)
            l_run = rs if l_run is None else l_run + rs
            acc = pv if acc is None else acc + pv
        o_ref[0, rows, :] = (acc / l_run).astype(o_ref.dtype)


def moba_attention(z3, slopes):
    batch = z3.shape[0]
    cq, ck, cv = OFF_QA // HEAD_DIM, OFF_KA // HEAD_DIM, OFF_VA // HEAD_DIM
    return pl.pallas_call(
        _moba_kernel,
        grid=(batch, MOBA_HEADS),
        in_specs=[
            pl.BlockSpec(memory_space=pltpu.SMEM),
            pl.BlockSpec((1, SEQ, HEAD_DIM), lambda b, h: (b, 0, cq + h)),
            pl.BlockSpec((1, SEQ, HEAD_DIM), lambda b, h: (b, 0, ck + h)),
            pl.BlockSpec((1, SEQ, HEAD_DIM), lambda b, h: (b, 0, cv + h)),
        ],
        out_specs=pl.BlockSpec((1, SEQ, HEAD_DIM), lambda b, h: (b, 0, h)),
        out_shape=jax.ShapeDtypeStruct((batch, SEQ, MOBA_WIDTH), BF16),
        scratch_shapes=[pltpu.VMEM((MOBA_BLOCK, SEQ), F32)],
        compiler_params=_params(("arbitrary", "arbitrary")),
        name="moba_attention",
    )(slopes, z3, z3, z3)


def _shift_rows(x, k):
    rolled = pltpu.roll(x, k, 0)
    row = lax.broadcasted_iota(jnp.int32, x.shape, 0)
    return jnp.where(row < k, 0.0, rolled)


def _gconv_kernel(bx_ref, bg_ref, cg_ref, w_ref, o_ref):
    xin = cg_ref[0].astype(F32) * bx_ref[0].astype(F32)
    w = w_ref[...]
    y = w[2:3] * xin + w[1:2] * _shift_rows(xin, 1) + w[0:1] * _shift_rows(xin, 2)
    o_ref[0] = (bg_ref[0].astype(F32) * y).astype(o_ref.dtype)


def gated_conv(z3, conv_w):
    tc = 256
    batch = z3.shape[0]
    cx, cg_, cc = OFF_BX // tc, OFF_BG // tc, OFF_CG // tc
    return pl.pallas_call(
        _gconv_kernel,
        grid=(batch, CONV_WIDTH // tc),
        in_specs=[
            pl.BlockSpec((1, SEQ, tc), lambda b, j: (b, 0, cx + j)),
            pl.BlockSpec((1, SEQ, tc), lambda b, j: (b, 0, cg_ + j)),
            pl.BlockSpec((1, SEQ, tc), lambda b, j: (b, 0, cc + j)),
            pl.BlockSpec((3, tc), lambda b, j: (0, j)),
        ],
        out_specs=pl.BlockSpec((1, SEQ, tc), lambda b, j: (b, 0, j)),
        out_shape=jax.ShapeDtypeStruct((batch, SEQ, CONV_WIDTH), BF16),
        compiler_params=_params(("arbitrary", "arbitrary")),
        name="gated_conv",
    )(z3, z3, z3, conv_w)


GLA_GROUP = 256


def _gla_kernel(q_ref, k_ref, v_ref, r_ref, ac_ref, a2_ref, ab_ref, ng_ref, o_ref):
    cl, gr = GLA_CHUNK, GLA_GROUP

    lane = lax.broadcasted_iota(jnp.int32, (1, 2 * GLA_DK), 1)
    head0 = jnp.where(lane < GLA_DK, 1.0, 0.0)
    head1 = 1.0 - head0
    ri = lax.broadcasted_iota(jnp.int32, (gr, gr), 0)
    ci = lax.broadcasted_iota(jnp.int32, (gr, gr), 1)
    same_chunk = lax.shift_right_logical(ri, 6) == lax.shift_right_logical(ci, 6)
    causal = jnp.logical_and(same_chunk, ci <= ri)
    tril = jnp.where(causal, 1.0, 0.0).astype(BF16)
    ones = jnp.where(same_chunk, 1.0, 0.0).astype(BF16)
    sr = lax.broadcasted_iota(jnp.int32, (2 * GLA_DV, 2 * GLA_DK), 0)
    sc = lax.broadcasted_iota(jnp.int32, (2 * GLA_DV, 2 * GLA_DK), 1)
    same_head = jnp.where((sr < GLA_DV) == (sc < GLA_DK), 1.0, 0.0)

    a2h, a2l = _split_bf16(a2_ref[...], 2)
    a_bias = ab_ref[...]
    ng = ng_ref[...]

    st = jnp.zeros((2 * GLA_DV, 2 * GLA_DK), F32)
    for grp in range(SEQ // gr):
        rows = slice(grp * gr, (grp + 1) * gr)
        ah, al = _split_bf16(ac_ref[0, rows, :], 2)
        zz = _dot(ah, a2h) + _dot(ah, a2l) + _dot(al, a2h) + a_bias
        g = (jnp.minimum(zz, 0.0) - jnp.log1p(jnp.exp(-jnp.abs(zz)))) * (1.0 / GLA_TAU)
        g0, g1, g2 = _split_bf16(g, 3)
        bcum = _dot(tril, g0) + _dot(tril, g1) + _dot(tril, g2)
        btot = _dot(ones, g0) + _dot(ones, g1) + _dot(ones, g2)
        qf = q_ref[0, rows, :].astype(F32) * (GLA_DK ** -0.5)
        kf = k_ref[0, rows, :].astype(F32)
        q_i = qf * jnp.exp(bcum)
        k_i = (kf * jnp.exp(-bcum)).astype(BF16)
        k_dec = (kf * jnp.exp(btot - bcum)).astype(BF16)
        dec = jnp.exp(btot)
        v = v_ref[0, rows, :]

        halves = []
        for hh, hm in enumerate((head0, head1)):
            att = _dot_nt((q_i * hm).astype(BF16), k_i)
            att = jnp.where(causal, att, 0.0).astype(BF16)
            halves.append(_dot(att, v[:, hh * GLA_DV:(hh + 1) * GLA_DV]))
        o_intra = jnp.concatenate(halves, axis=1)

        q_b = q_i.astype(BF16)
        inter = []
        for c in range(gr // cl):
            cr = slice(c * cl, (c + 1) * cl)
            inter.append(_dot_nt(q_b[cr], st.astype(BF16)))
            st = st * dec[c * cl:c * cl + 1, :] + _dot_tn(v[cr], k_dec[cr]) * same_head
        o = o_intra + jnp.concatenate(inter, axis=0)

        outs = []
        for hh in range(2):
            oh = o[:, hh * GLA_DV:(hh + 1) * GLA_DV]
            ms = jnp.mean(oh * oh, axis=-1, keepdims=True)
            outs.append(oh * lax.rsqrt(ms + NORM_EPS) * ng)
        y = jnp.concatenate(outs, axis=1)
        o_ref[0, rows, :] = (_silu(r_ref[0, rows, :].astype(F32)) * y).astype(o_ref.dtype)


def gla_mixer(z3, ac3, a2_pad, a_bias, norm_g):
    batch = z3.shape[0]
    kw, vw = 2 * GLA_DK, 2 * GLA_DV
    cq, ck, cv, cr = OFF_QC // kw, OFF_KC // kw, OFF_VC // vw, OFF_RC // vw
    return pl.pallas_call(
        _gla_kernel,
        grid=(batch, GLA_HEADS // 2),
        in_specs=[
            pl.BlockSpec((1, SEQ, kw), lambda b, j: (b, 0, cq + j)),
            pl.BlockSpec((1, SEQ, kw), lambda b, j: (b, 0, ck + j)),
            pl.BlockSpec((1, SEQ, vw), lambda b, j: (b, 0, cv + j)),
            pl.BlockSpec((1, SEQ, vw), lambda b, j: (b, 0, cr + j)),
            pl.BlockSpec((1, SEQ, LANES), lambda b, j: (b, 0, 0)),
            pl.BlockSpec((LANES, kw), lambda b, j: (0, j)),
            pl.BlockSpec((1, kw), lambda b, j: (0, j)),
            pl.BlockSpec((1, GLA_DV), lambda b, j: (0, 0)),
        ],
        out_specs=pl.BlockSpec((1, SEQ, vw), lambda b, j: (b, 0, j)),
        out_shape=jax.ShapeDtypeStruct((batch, SEQ, GLA_VW), BF16),
        compiler_params=_params(("arbitrary", "arbitrary")),
        name="gla_mixer",
    )(z3, z3, z3, z3, ac3, a2_pad, a_bias.reshape(1, GLA_KW), norm_g.reshape(1, GLA_DV))


def _merge_kernel(h_ref, ya_ref, yb_ref, yc_ref, wg0_ref, wg1_ref, wg2_ref,
                  wa_ref, wb_ref, wc_ref, o_ref):
    h = h_ref[...]
    acc = jax.nn.sigmoid(_dot_nt(h, wg0_ref[...])) * _dot(ya_ref[...], wa_ref[...].astype(BF16))
    acc += jax.nn.sigmoid(_dot_nt(h, wg1_ref[...])) * _dot(yb_ref[...], wb_ref[...].astype(BF16))
    acc += jax.nn.sigmoid(_dot_nt(h, wg2_ref[...])) * _dot(yc_ref[...], wc_ref[...].astype(BF16))
    o_ref[...] = acc.astype(o_ref.dtype)


def merge_branches(h2, ya, yb, yc, w_gl, wa, wb, wc, layer):
    tm, tn = 1024, 256
    n_tok = h2.shape[0]
    nj = D_MODEL // tn
    row = lambda i, j: (i, 0)
    colw = lambda i, j: (layer, 0, j)
    return pl.pallas_call(
        _merge_kernel,
        grid=(n_tok // tm, nj),
        in_specs=[
            pl.BlockSpec((tm, D_MODEL), row, pipeline_mode=RESIDENT),
            pl.BlockSpec((tm, MOBA_WIDTH), row, pipeline_mode=RESIDENT),
            pl.BlockSpec((tm, CONV_WIDTH), row, pipeline_mode=RESIDENT),
            pl.BlockSpec((tm, GLA_VW), row, pipeline_mode=RESIDENT),
            pl.BlockSpec((tn, D_MODEL), lambda i, j: (j, 0)),
            pl.BlockSpec((tn, D_MODEL), lambda i, j: (nj + j, 0)),
            pl.BlockSpec((tn, D_MODEL), lambda i, j: (2 * nj + j, 0)),
            pl.BlockSpec((None, MOBA_WIDTH, tn), colw),
            pl.BlockSpec((None, CONV_WIDTH, tn), colw),
            pl.BlockSpec((None, GLA_VW, tn), colw),
        ],
        out_specs=pl.BlockSpec((tm, tn), lambda i, j: (i, j)),
        out_shape=jax.ShapeDtypeStruct((n_tok, D_MODEL), BF16),
        compiler_params=_params(("arbitrary", "arbitrary")),
        name="merge_branches",
    )(h2, ya, yb, yc, w_gl, w_gl, w_gl, wa, wb, wc)


def _proj_residual_kernel(a_ref, w_ref, x_ref, gt_ref, o_ref):
    o_ref[...] = x_ref[...] + gt_ref[0] * _dot(a_ref[...], w_ref[...].astype(BF16))


def proj_residual(a, w, layer, x2, gt, tm, tn, name):
    m, k = a.shape
    n = w.shape[-1]
    per_batch = SEQ // tm
    return pl.pallas_call(
        _proj_residual_kernel,
        grid=(m // tm, n // tn),
        in_specs=[
            pl.BlockSpec((tm, k), lambda i, j: (i, 0), pipeline_mode=RESIDENT),
            pl.BlockSpec((None, k, tn), lambda i, j: (layer, 0, j)),
            pl.BlockSpec((tm, tn), lambda i, j: (i, j)),
            pl.BlockSpec((1, 1, tn), lambda i, j: (i // per_batch, 0, j)),
        ],
        out_specs=pl.BlockSpec((tm, tn), lambda i, j: (i, j)),
        out_shape=jax.ShapeDtypeStruct((m, n), F32),
        compiler_params=_params(("arbitrary", "arbitrary")),
        name=name,
    )(a, w, x2, gt)


def _ffn_up_kernel(a_ref, wg_ref, wu_ref, cw_ref, o_ref, *, row_chunk):
    wg = wg_ref[...].astype(BF16)
    wu = wu_ref[...].astype(BF16)
    cw = cw_ref[...]
    prev = jnp.zeros((8, wg.shape[1]), F32)
    row = lax.broadcasted_iota(jnp.int32, (row_chunk, wg.shape[1]), 0)
    for r in range(a_ref.shape[0] // row_chunk):
        rows = slice(r * row_chunk, (r + 1) * row_chunk)
        a = a_ref[rows, :]
        g = _dot(a, wg)
        up = _dot(a, wu)
        s1 = jnp.where(row == 0, prev[7:8], pltpu.roll(g, 1, 0))
        s2 = jnp.where(row == 0, prev[6:7], jnp.where(row == 1, prev[7:8], pltpu.roll(g, 2, 0)))
        u = cw[2:3] * g + cw[1:2] * s1 + cw[0:1] * s2
        o_ref[rows, :] = (_silu(u) * up).astype(o_ref.dtype)
        prev = g[row_chunk - 8:row_chunk, :]


def ffn_up(h2, wg, wu, conv_w, layer):
    tm, tn = SEQ, 256
    n_tok = h2.shape[0]
    return pl.pallas_call(
        functools.partial(_ffn_up_kernel, row_chunk=512),
        grid=(n_tok // tm, D_FF // tn),
        in_specs=[
            pl.BlockSpec((tm, D_MODEL), lambda i, j: (i, 0), pipeline_mode=RESIDENT),
            pl.BlockSpec((None, D_MODEL, tn), lambda i, j: (layer, 0, j)),
            pl.BlockSpec((None, D_MODEL, tn), lambda i, j: (layer, 0, j)),
            pl.BlockSpec((None, 3, tn), lambda i, j: (layer, 0, j)),
        ],
        out_specs=pl.BlockSpec((tm, tn), lambda i, j: (i, j)),
        out_shape=jax.ShapeDtypeStruct((n_tok, D_FF), BF16),
        compiler_params=_params(("arbitrary", "arbitrary")),
        name="ffn_up",
    )(h2, wg, wu, conv_w)


def kernel(x, c, norm1_g, w_ada, b_ada, w_in, conv_w, gla_a2, gla_a_bias, gla_norm_g,
           w_branch_a, w_branch_b, w_branch_c, w_out, norm2_g, w_ffn_gate, w_ffn_up,
           ffn_conv_w, w_ffn_down, final_norm_g):
    batch = x.shape[0]
    n_tok = batch * SEQ
    slopes = alibi_slopes(MOBA_HEADS)
    c_pad = jnp.zeros((8, D_MODEL), F32).at[:batch].set(c)
    mod = ada_modulation(c_pad, w_ada, b_ada)
    mod = mod[:, :batch].reshape(DEPTH, batch, 6, 1, D_MODEL)
    w_in_t = jnp.swapaxes(w_in, 1, 2)

    for l in range(DEPTH):
        sh1, sc1, gt1, sh2, sc2, gt2 = (mod[l, :, t] for t in range(6))

        w_gl = gate_weights(w_in_t, l)
        a2_pad = jnp.zeros((LANES, GLA_KW), F32).at[:GLA_RANK].set(gla_a2[l])

        h = norm_modulate(x, norm1_g[l], sc1, sh1)
        h2 = h.reshape(n_tok, D_MODEL)
        z = in_proj(h2, w_in_t, l, 0, Z_WIDTH, BF16, 2048, 512, "in_proj")
        ac = in_proj(h2, w_in_t, l, OFF_AC, LANES, F32, 2048, LANES, "in_proj_rank")
        z3 = z.reshape(batch, SEQ, Z_WIDTH)

        ya = moba_attention(z3, slopes)
        yb = gated_conv(z3, conv_w[l])
        yc = gla_mixer(z3, ac.reshape(batch, SEQ, LANES), a2_pad, gla_a_bias[l], gla_norm_g[l])

        merged = merge_branches(
            h2, ya.reshape(n_tok, MOBA_WIDTH), yb.reshape(n_tok, CONV_WIDTH),
            yc.reshape(n_tok, GLA_VW), w_gl, w_branch_a, w_branch_b, w_branch_c, l)
        x2 = proj_residual(merged, w_out, l, x.reshape(n_tok, D_MODEL), gt1, 2048, 512, "out_proj")
        x = x2.reshape(batch, SEQ, D_MODEL)

        h = norm_modulate(x, norm2_g[l], sc2, sh2)
        act = ffn_up(h.reshape(n_tok, D_MODEL), w_ffn_gate, w_ffn_up, ffn_conv_w, l)
        x2 = proj_residual(act, w_ffn_down, l, x2, gt2, 1024, 256, "ffn_down")
        x = x2.reshape(batch, SEQ, D_MODEL)

    return final_norm(x, final_norm_g)
```

```python
import functools
import math

import numpy as np
import jax
import jax.numpy as jnp
from jax import lax
from jax.experimental import pallas as pl
from jax.experimental.pallas import tpu as pltpu

F32 = jnp.float32
BF16 = jnp.bfloat16

D_MODEL = 4096
SEQ = 2048
DEPTH = 2

HEAD_DIM = 128
MOBA_HEADS = 12
MOBA_WIDTH = MOBA_HEADS * HEAD_DIM
MOBA_BLOCK = 256
MOBA_TOPK = 3
N_MOBA_BLOCKS = SEQ // MOBA_BLOCK
CONV_WIDTH = 1024
GLA_HEADS = 12
GLA_DK = 64
GLA_DV = 128
GLA_KW = GLA_HEADS * GLA_DK
GLA_VW = GLA_HEADS * GLA_DV
GLA_RANK = 16
GLA_TAU = 16.0
GLA_CHUNK = 64
D_FF = 11008
NORM_EPS = 1e-6

OFF_QA = 0
OFF_KA = OFF_QA + MOBA_WIDTH
OFF_VA = OFF_KA + MOBA_WIDTH
OFF_BX = OFF_VA + MOBA_WIDTH
OFF_BG = OFF_BX + CONV_WIDTH
OFF_CG = OFF_BG + CONV_WIDTH
OFF_QC = OFF_CG + CONV_WIDTH
OFF_KC = OFF_QC + GLA_KW
OFF_VC = OFF_KC + GLA_KW
OFF_RC = OFF_VC + GLA_VW
OFF_AC = OFF_RC + GLA_VW
OFF_GATE = OFF_AC + GLA_RANK
Z_WIDTH = OFF_AC

LANES = 128
VMEM_LIMIT = 56 * 1024 * 1024
RESIDENT = pl.Buffered(1)


def _params(sem):
    return pltpu.CompilerParams(dimension_semantics=sem, vmem_limit_bytes=VMEM_LIMIT)


def _dot(a, b):
    return jnp.dot(a, b, preferred_element_type=F32)


def _dot_nt(a, b):
    return lax.dot_general(a, b, (((1,), (1,)), ((), ())), preferred_element_type=F32)


def _dot_tn(a, b):
    return lax.dot_general(a, b, (((0,), (0,)), ((), ())), preferred_element_type=F32)


def _split_bf16(x, parts):
    out = []
    r = x
    for _ in range(parts):
        p = r.astype(BF16)
        out.append(p)
        r = r - p.astype(F32)
    return out


def _silu(x):
    return x * jax.nn.sigmoid(x)


def alibi_slopes(n):
    def pow2(m):
        start = 2.0 ** (-(2.0 ** -(math.log2(m) - 3)))
        return [start ** (i + 1) for i in range(m)]
    if math.log2(n).is_integer():
        s = pow2(n)
    else:
        closest = 2 ** math.floor(math.log2(n))
        s = pow2(closest) + pow2(2 * closest)[0::2][: n - closest]
    return jnp.asarray(np.array(s, dtype=np.float32))


def _ada_kernel(c_ref, w_ref, b_ref, o_ref):
    ca = _silu(c_ref[...]).astype(BF16)
    o_ref[0] = _dot(ca, w_ref[0].astype(BF16)) + b_ref[0]


def ada_modulation(c_pad, w_ada, b_ada):
    n_out = w_ada.shape[-1]
    tn = 512
    return pl.pallas_call(
        _ada_kernel,
        grid=(DEPTH, n_out // tn),
        in_specs=[
            pl.BlockSpec((8, D_MODEL), lambda l, j: (0, 0)),
            pl.BlockSpec((1, D_MODEL, tn), lambda l, j: (l, 0, j)),
            pl.BlockSpec((1, 1, tn), lambda l, j: (l, 0, j)),
        ],
        out_specs=pl.BlockSpec((1, 8, tn), lambda l, j: (l, 0, j)),
        out_shape=jax.ShapeDtypeStruct((DEPTH, 8, n_out), F32),
        compiler_params=_params(("arbitrary", "arbitrary")),
        name="ada_modulation",
    )(c_pad, w_ada, b_ada.reshape(DEPTH, 1, n_out))


def _norm_mod_kernel(x_ref, g_ref, sc_ref, sh_ref, o_ref):
    x = x_ref[0]
    y = x * lax.rsqrt(jnp.mean(x * x, axis=-1, keepdims=True) + NORM_EPS) * g_ref[...]
    o_ref[0] = (y * (1.0 + sc_ref[0]) + sh_ref[0]).astype(o_ref.dtype)


def _norm_kernel(x_ref, g_ref, o_ref):
    x = x_ref[0]
    y = x * lax.rsqrt(jnp.mean(x * x, axis=-1, keepdims=True) + NORM_EPS) * g_ref[...]
    o_ref[0] = y.astype(o_ref.dtype)


def norm_modulate(x, g, sc, sh):
    tm = 256
    batch = x.shape[0]
    return pl.pallas_call(
        _norm_mod_kernel,
        grid=(batch, SEQ // tm),
        in_specs=[
            pl.BlockSpec((1, tm, D_MODEL), lambda b, i: (b, i, 0)),
            pl.BlockSpec((1, D_MODEL), lambda b, i: (0, 0)),
            pl.BlockSpec((1, 1, D_MODEL), lambda b, i: (b, 0, 0)),
            pl.BlockSpec((1, 1, D_MODEL), lambda b, i: (b, 0, 0)),
        ],
        out_specs=pl.BlockSpec((1, tm, D_MODEL), lambda b, i: (b, i, 0)),
        out_shape=jax.ShapeDtypeStruct((batch, SEQ, D_MODEL), BF16),
        compiler_params=_params(("arbitrary", "arbitrary")),
        name="norm_modulate",
    )(x, g.reshape(1, D_MODEL), sc, sh)


def final_norm(x, g):
    tm = 256
    batch = x.shape[0]
    return pl.pallas_call(
        _norm_kernel,
        grid=(batch, SEQ // tm),
        in_specs=[
            pl.BlockSpec((1, tm, D_MODEL), lambda b, i: (b, i, 0)),
            pl.BlockSpec((1, D_MODEL), lambda b, i: (0, 0)),
        ],
        out_specs=pl.BlockSpec((1, tm, D_MODEL), lambda b, i: (b, i, 0)),
        out_shape=jax.ShapeDtypeStruct((batch, SEQ, D_MODEL), F32),
        compiler_params=_params(("arbitrary", "arbitrary")),
        name="final_norm",
    )(x, g.reshape(1, D_MODEL))


def _mm_kernel(a_ref, w_ref, o_ref):
    o_ref[...] = _dot_nt(a_ref[...], w_ref[...].astype(BF16)).astype(o_ref.dtype)


def in_proj(a, w_in_t, layer, col0, n, out_dtype, tm, tn, name):
    m, k = a.shape
    j0 = col0 // tn
    return pl.pallas_call(
        _mm_kernel,
        grid=(m // tm, n // tn),
        in_specs=[
            pl.BlockSpec((tm, k), lambda i, j: (i, 0), pipeline_mode=RESIDENT),
            pl.BlockSpec((None, tn, k), lambda i, j: (layer, j0 + j, 0)),
        ],
        out_specs=pl.BlockSpec((tm, tn), lambda i, j: (i, j)),
        out_shape=jax.ShapeDtypeStruct((m, n), out_dtype),
        compiler_params=_params(("arbitrary", "arbitrary")),
        name=name,
    )(a, w_in_t)


def _in_proj_main_kernel(a_ref, w_ref, gw_ref, o_ref, gwo_ref):
    o_ref[...] = _dot_nt(a_ref[...], w_ref[...].astype(BF16)).astype(o_ref.dtype)
    gwo_ref[...] = gw_ref[0].astype(gwo_ref.dtype)


def in_proj_main(a, w_in_t, layer):
    tm, tn = 2048, 512
    m, k = a.shape
    ni, nj = m // tm, Z_WIDTH // tn
    n_gate = 3 * D_MODEL
    slab = n_gate // (ni * nj)
    assert slab * ni * nj == n_gate and slab % 16 == 0
    return pl.pallas_call(
        _in_proj_main_kernel,
        grid=(ni, nj),
        in_specs=[
            pl.BlockSpec((tm, k), lambda i, j: (i, 0), pipeline_mode=RESIDENT),
            pl.BlockSpec((None, tn, k), lambda i, j: (layer, j, 0)),
            pl.BlockSpec((pl.Element(1), pl.Element(slab), pl.Element(k)),
                         lambda i, j: (layer, pl.multiple_of(OFF_GATE + (i * nj + j) * slab, GLA_RANK), 0)),
        ],
        out_specs=[
            pl.BlockSpec((tm, tn), lambda i, j: (i, j)),
            pl.BlockSpec((slab, k), lambda i, j: (i * nj + j, 0)),
        ],
        out_shape=[
            jax.ShapeDtypeStruct((m, Z_WIDTH), BF16),
            jax.ShapeDtypeStruct((n_gate, k), BF16),
        ],
        compiler_params=_params(("arbitrary", "arbitrary")),
        name="in_proj",
    )(a, w_in_t, w_in_t)


def _moba_kernel(slopes_ref, q_ref, k_ref, v_ref, o_ref, s_ref):
    slope = slopes_ref[pl.program_id(1)]
    scale = HEAD_DIM ** -0.5
    blk = MOBA_BLOCK

    r = lax.broadcasted_iota(jnp.int32, (LANES, SEQ), 0)
    t = lax.broadcasted_iota(jnp.int32, (LANES, SEQ), 1)
    pm = jnp.where(lax.shift_right_logical(t, 8) == r, 1.0 / blk, 0.0).astype(BF16)
    kmh, kml = _split_bf16(_dot(pm, k_ref[0]), 2)

    row = lax.broadcasted_iota(jnp.int32, (blk, blk), 0)
    col = lax.broadcasted_iota(jnp.int32, (blk, blk), 1)
    srel = slope * (col - row).astype(F32)
    causal = col <= row
    eye = jnp.where(row == col, 1.0, 0.0).astype(BF16)
    blk_id = lax.broadcasted_iota(jnp.int32, (N_MOBA_BLOCKS, blk), 0)

    for qi in range(N_MOBA_BLOCKS):
        rows = slice(qi * blk, (qi + 1) * blk)
        q_raw = q_ref[0, rows, :]
        qs = (q_raw.astype(F32) * scale).astype(BF16)

        sel_b = None
        if qi > MOBA_TOPK:
            g = (_dot_nt(kmh, q_raw) + _dot_nt(kml, q_raw))[:N_MOBA_BLOCKS] * scale
            rank = jnp.zeros_like(g)
            for m in range(qi):
                gm = g[m:m + 1, :]
                tie = jnp.where(blk_id > m, 1.0, 0.0)
                rank = rank + jnp.where(gm > g, 1.0, jnp.where(gm == g, tie, 0.0))
            sel = jnp.where(rank < float(MOBA_TOPK), 1.0, 0.0)
            sel_b = [_dot_nt(eye, jnp.broadcast_to(sel[n:n + 1, :], (LANES, blk)).astype(BF16))
                     for n in range(qi)]

        m_run = None
        for n in range(qi + 1):
            cols = slice(n * blk, (n + 1) * blk)
            s = _dot_nt(qs, k_ref[0, cols, :]) + srel
            if n == qi:
                s = jnp.where(causal, s, -jnp.inf)
            s_ref[:, cols] = s
            rm = jnp.max(s, axis=1, keepdims=True) - slope * float(blk * (qi - n))
            if sel_b is not None and n < qi:
                rm = jnp.where(sel_b[n][:, 0:1] > 0.5, rm, -jnp.inf)
            m_run = rm if m_run is None else jnp.maximum(m_run, rm)

        l_run = None
        acc = None
        for n in range(qi + 1):
            cols = slice(n * blk, (n + 1) * blk)
            p = jnp.exp(s_ref[:, cols] - (m_run + slope * float(blk * (qi - n))))
            rs = jnp.sum(p, axis=1, keepdims=True)
            pv = _dot(p.astype(BF16), v_ref[0, cols, :])
            if sel_b is not None and n < qi:
                rs = jnp.where(sel_b[n][:, 0:1] > 0.5, rs, 0.0)
                pv = jnp.where(sel_b[n] > 0.5, pv, 0.0)
            l_run = rs if l_run is None else l_run + rs
            acc = pv if acc is None else acc + pv
        o_ref[0, rows, :] = (acc / l_run).astype(o_ref.dtype)


def moba_attention(z3, slopes):
    batch = z3.shape[0]
    cq, ck, cv = OFF_QA // HEAD_DIM, OFF_KA // HEAD_DIM, OFF_VA // HEAD_DIM
    return pl.pallas_call(
        _moba_kernel,
        grid=(batch, MOBA_HEADS),
        in_specs=[
            pl.BlockSpec(memory_space=pltpu.SMEM),
            pl.BlockSpec((1, SEQ, HEAD_DIM), lambda b, h: (b, 0, cq + h)),
            pl.BlockSpec((1, SEQ, HEAD_DIM), lambda b, h: (b, 0, ck + h)),
            pl.BlockSpec((1, SEQ, HEAD_DIM), lambda b, h: (b, 0, cv + h)),
        ],
        out_specs=pl.BlockSpec((1, SEQ, HEAD_DIM), lambda b, h: (b, 0, h)),
        out_shape=jax.ShapeDtypeStruct((batch, SEQ, MOBA_WIDTH), BF16),
        scratch_shapes=[pltpu.VMEM((MOBA_BLOCK, SEQ), F32)],
        compiler_params=_params(("arbitrary", "arbitrary")),
        name="moba_attention",
    )(slopes, z3, z3, z3)


def _shift_rows(x, k):
    rolled = pltpu.roll(x, k, 0)
    row = lax.broadcasted_iota(jnp.int32, x.shape, 0)
    return jnp.where(row < k, 0.0, rolled)


def _gconv_kernel(bx_ref, bg_ref, cg_ref, w_ref, o_ref):
    xin = cg_ref[0].astype(F32) * bx_ref[0].astype(F32)
    w = w_ref[...]
    y = w[2:3] * xin + w[1:2] * _shift_rows(xin, 1) + w[0:1] * _shift_rows(xin, 2)
    o_ref[0] = (bg_ref[0].astype(F32) * y).astype(o_ref.dtype)


def gated_conv(z3, conv_w):
    tc = 256
    batch = z3.shape[0]
    cx, cg_, cc = OFF_BX // tc, OFF_BG // tc, OFF_CG // tc
    return pl.pallas_call(
        _gconv_kernel,
        grid=(batch, CONV_WIDTH // tc),
        in_specs=[
            pl.BlockSpec((1, SEQ, tc), lambda b, j: (b, 0, cx + j)),
            pl.BlockSpec((1, SEQ, tc), lambda b, j: (b, 0, cg_ + j)),
            pl.BlockSpec((1, SEQ, tc), lambda b, j: (b, 0, cc + j)),
            pl.BlockSpec((3, tc), lambda b, j: (0, j)),
        ],
        out_specs=pl.BlockSpec((1, SEQ, tc), lambda b, j: (b, 0, j)),
        out_shape=jax.ShapeDtypeStruct((batch, SEQ, CONV_WIDTH), BF16),
        compiler_params=_params(("arbitrary", "arbitrary")),
        name="gated_conv",
    )(z3, z3, z3, conv_w)


GLA_GROUP = 256


def _gla_kernel(q_ref, k_ref, v_ref, r_ref, ac_ref, a2_ref, ab_ref, ng_ref, o_ref):
    cl, gr = GLA_CHUNK, GLA_GROUP

    lane = lax.broadcasted_iota(jnp.int32, (1, 2 * GLA_DK), 1)
    head0 = jnp.where(lane < GLA_DK, 1.0, 0.0)
    head1 = 1.0 - head0
    ri = lax.broadcasted_iota(jnp.int32, (gr, gr), 0)
    ci = lax.broadcasted_iota(jnp.int32, (gr, gr), 1)
    same_chunk = lax.shift_right_logical(ri, 6) == lax.shift_right_logical(ci, 6)
    causal = jnp.logical_and(same_chunk, ci <= ri)
    tril = jnp.where(causal, 1.0, 0.0).astype(BF16)
    ones = jnp.where(same_chunk, 1.0, 0.0).astype(BF16)
    sr = lax.broadcasted_iota(jnp.int32, (2 * GLA_DV, 2 * GLA_DK), 0)
    sc = lax.broadcasted_iota(jnp.int32, (2 * GLA_DV, 2 * GLA_DK), 1)
    same_head = jnp.where((sr < GLA_DV) == (sc < GLA_DK), 1.0, 0.0)

    a2h, a2l = _split_bf16(a2_ref[...], 2)
    a_bias = ab_ref[...]
    ng = ng_ref[...]

    st = jnp.zeros((2 * GLA_DV, 2 * GLA_DK), F32)
    for grp in range(SEQ // gr):
        rows = slice(grp * gr, (grp + 1) * gr)
        ah, al = _split_bf16(ac_ref[0, rows, :], 2)
        zz = _dot(ah, a2h) + _dot(ah, a2l) + _dot(al, a2h) + a_bias
        g = (jnp.minimum(zz, 0.0) - jnp.log1p(jnp.exp(-jnp.abs(zz)))) * (1.0 / GLA_TAU)
        g0, g1, g2 = _split_bf16(g, 3)
        bcum = _dot(tril, g0) + _dot(tril, g1) + _dot(tril, g2)
        btot = _dot(ones, g0) + _dot(ones, g1) + _dot(ones, g2)
        qf = q_ref[0, rows, :].astype(F32) * (GLA_DK ** -0.5)
        kf = k_ref[0, rows, :].astype(F32)
        q_i = qf * jnp.exp(bcum)
        k_i = (kf * jnp.exp(-bcum)).astype(BF16)
        k_dec = (kf * jnp.exp(btot - bcum)).astype(BF16)
        dec = jnp.exp(btot)
        v = v_ref[0, rows, :]

        halves = []
        for hh, hm in enumerate((head0, head1)):
            att = _dot_nt((q_i * hm).astype(BF16), k_i)
            att = jnp.where(causal, att, 0.0).astype(BF16)
            halves.append(_dot(att, v[:, hh * GLA_DV:(hh + 1) * GLA_DV]))
        o_intra = jnp.concatenate(halves, axis=1)

        q_b = q_i.astype(BF16)
        inter = []
        for c in range(gr // cl):
            cr = slice(c * cl, (c + 1) * cl)
            inter.append(_dot_nt(q_b[cr], st.astype(BF16)))
            st = st * dec[c * cl:c * cl + 1, :] + _dot_tn(v[cr], k_dec[cr]) * same_head
        o = o_intra + jnp.concatenate(inter, axis=0)

        outs = []
        for hh in range(2):
            oh = o[:, hh * GLA_DV:(hh + 1) * GLA_DV]
            ms = jnp.mean(oh * oh, axis=-1, keepdims=True)
            outs.append(oh * lax.rsqrt(ms + NORM_EPS) * ng)
        y = jnp.concatenate(outs, axis=1)
        o_ref[0, rows, :] = (_silu(r_ref[0, rows, :].astype(F32)) * y).astype(o_ref.dtype)


def gla_mixer(z3, ac3, a2_pad, a_bias, norm_g):
    batch = z3.shape[0]
    kw, vw = 2 * GLA_DK, 2 * GLA_DV
    cq, ck, cv, cr = OFF_QC // kw, OFF_KC // kw, OFF_VC // vw, OFF_RC // vw
    return pl.pallas_call(
        _gla_kernel,
        grid=(batch, GLA_HEADS // 2),
        in_specs=[
            pl.BlockSpec((1, SEQ, kw), lambda b, j: (b, 0, cq + j)),
            pl.BlockSpec((1, SEQ, kw), lambda b, j: (b, 0, ck + j)),
            pl.BlockSpec((1, SEQ, vw), lambda b, j: (b, 0, cv + j)),
            pl.BlockSpec((1, SEQ, vw), lambda b, j: (b, 0, cr + j)),
            pl.BlockSpec((1, SEQ, LANES), lambda b, j: (b, 0, 0)),
            pl.BlockSpec((LANES, kw), lambda b, j: (0, j)),
            pl.BlockSpec((1, kw), lambda b, j: (0, j)),
            pl.BlockSpec((1, GLA_DV), lambda b, j: (0, 0)),
        ],
        out_specs=pl.BlockSpec((1, SEQ, vw), lambda b, j: (b, 0, j)),
        out_shape=jax.ShapeDtypeStruct((batch, SEQ, GLA_VW), BF16),
        compiler_params=_params(("arbitrary", "arbitrary")),
        name="gla_mixer",
    )(z3, z3, z3, z3, ac3, a2_pad, a_bias.reshape(1, GLA_KW), norm_g.reshape(1, GLA_DV))


def _merge_kernel(h_ref, ya_ref, yb_ref, yc_ref, wg0_ref, wg1_ref, wg2_ref,
                  wa_ref, wb_ref, wc_ref, wo_ref, o_ref, wob_ref):
    wob_ref[...] = wo_ref[...].astype(wob_ref.dtype)
    h = h_ref[...]
    acc = jax.nn.sigmoid(_dot_nt(h, wg0_ref[...])) * _dot(ya_ref[...], wa_ref[...].astype(BF16))
    acc += jax.nn.sigmoid(_dot_nt(h, wg1_ref[...])) * _dot(yb_ref[...], wb_ref[...].astype(BF16))
    acc += jax.nn.sigmoid(_dot_nt(h, wg2_ref[...])) * _dot(yc_ref[...], wc_ref[...].astype(BF16))
    o_ref[...] = acc.astype(o_ref.dtype)


def merge_branches(h2, ya, yb, yc, w_gl, wa, wb, wc, w_out, layer):
    tm, tn = 1024, 256
    n_tok = h2.shape[0]
    ni, nj = n_tok // tm, D_MODEL // tn
    slab = D_MODEL // (ni * nj)
    assert slab * ni * nj == D_MODEL and slab % 16 == 0
    row = lambda i, j: (i, 0)
    colw = lambda i, j: (layer, 0, j)
    return pl.pallas_call(
        _merge_kernel,
        grid=(ni, nj),
        in_specs=[
            pl.BlockSpec((tm, D_MODEL), row, pipeline_mode=RESIDENT),
            pl.BlockSpec((tm, MOBA_WIDTH), row, pipeline_mode=RESIDENT),
            pl.BlockSpec((tm, CONV_WIDTH), row, pipeline_mode=RESIDENT),
            pl.BlockSpec((tm, GLA_VW), row, pipeline_mode=RESIDENT),
            pl.BlockSpec((tn, D_MODEL), lambda i, j: (j, 0)),
            pl.BlockSpec((tn, D_MODEL), lambda i, j: (nj + j, 0)),
            pl.BlockSpec((tn, D_MODEL), lambda i, j: (2 * nj + j, 0)),
            pl.BlockSpec((None, MOBA_WIDTH, tn), colw),
            pl.BlockSpec((None, CONV_WIDTH, tn), colw),
            pl.BlockSpec((None, GLA_VW, tn), colw),
            pl.BlockSpec((None, slab, D_MODEL), lambda i, j: (layer, i * nj + j, 0)),
        ],
        out_specs=[
            pl.BlockSpec((tm, tn), lambda i, j: (i, j)),
            pl.BlockSpec((slab, D_MODEL), lambda i, j: (i * nj + j, 0)),
        ],
        out_shape=[
            jax.ShapeDtypeStruct((n_tok, D_MODEL), BF16),
            jax.ShapeDtypeStruct((D_MODEL, D_MODEL), BF16),
        ],
        compiler_params=_params(("arbitrary", "arbitrary")),
        name="merge_branches",
    )(h2, ya, yb, yc, w_gl, w_gl, w_gl, wa, wb, wc, w_out)


def _proj_residual_kernel(a_ref, w_ref, x_ref, gt_ref, o_ref):
    o_ref[...] = x_ref[...] + gt_ref[0] * _dot(a_ref[...], w_ref[...])


def proj_residual(a, w, x2, gt, tm, tn, name):
    m, k = a.shape
    n = w.shape[-1]
    per_batch = SEQ // tm
    return pl.pallas_call(
        _proj_residual_kernel,
        grid=(m // tm, n // tn),
        in_specs=[
            pl.BlockSpec((tm, k), lambda i, j: (i, 0), pipeline_mode=RESIDENT),
            pl.BlockSpec((k, tn), lambda i, j: (0, j)),
            pl.BlockSpec((tm, tn), lambda i, j: (i, j)),
            pl.BlockSpec((1, 1, tn), lambda i, j: (i // per_batch, 0, j)),
        ],
        out_specs=pl.BlockSpec((tm, tn), lambda i, j: (i, j)),
        out_shape=jax.ShapeDtypeStruct((m, n), F32),
        compiler_params=_params(("arbitrary", "arbitrary")),
        name=name,
    )(a, w, x2, gt)


def _ffn_up_kernel(a_ref, wg_ref, wu_ref, cw_ref, wd_ref, o_ref, wdb_ref, *, row_chunk):
    wdb_ref[...] = wd_ref[...].astype(wdb_ref.dtype)
    wg = wg_ref[...].astype(BF16)
    wu = wu_ref[...].astype(BF16)
    cw = cw_ref[...]
    prev = jnp.zeros((8, wg.shape[1]), F32)
    row = lax.broadcasted_iota(jnp.int32, (row_chunk, wg.shape[1]), 0)
    for r in range(a_ref.shape[0] // row_chunk):
        rows = slice(r * row_chunk, (r + 1) * row_chunk)
        a = a_ref[rows, :]
        g = _dot(a, wg)
        up = _dot(a, wu)
        s1 = jnp.where(row == 0, prev[7:8], pltpu.roll(g, 1, 0))
        s2 = jnp.where(row == 0, prev[6:7], jnp.where(row == 1, prev[7:8], pltpu.roll(g, 2, 0)))
        u = cw[2:3] * g + cw[1:2] * s1 + cw[0:1] * s2
        o_ref[rows, :] = (_silu(u) * up).astype(o_ref.dtype)
        prev = g[row_chunk - 8:row_chunk, :]


def ffn_up(h2, wg, wu, conv_w, w_down, layer):
    tm, tn = SEQ, 256
    n_tok = h2.shape[0]
    ni, nj = n_tok // tm, D_FF // tn
    slab = D_FF // (ni * nj)
    assert slab * ni * nj == D_FF and slab % 16 == 0
    return pl.pallas_call(
        functools.partial(_ffn_up_kernel, row_chunk=512),
        grid=(ni, nj),
        in_specs=[
            pl.BlockSpec((tm, D_MODEL), lambda i, j: (i, 0), pipeline_mode=RESIDENT),
            pl.BlockSpec((None, D_MODEL, tn), lambda i, j: (layer, 0, j)),
            pl.BlockSpec((None, D_MODEL, tn), lambda i, j: (layer, 0, j)),
            pl.BlockSpec((None, 3, tn), lambda i, j: (layer, 0, j)),
            pl.BlockSpec((None, slab, D_MODEL), lambda i, j: (layer, i * nj + j, 0)),
        ],
        out_specs=[
            pl.BlockSpec((tm, tn), lambda i, j: (i, j)),
            pl.BlockSpec((slab, D_MODEL), lambda i, j: (i * nj + j, 0)),
        ],
        out_shape=[
            jax.ShapeDtypeStruct((n_tok, D_FF), BF16),
            jax.ShapeDtypeStruct((D_FF, D_MODEL), BF16),
        ],
        compiler_params=_params(("arbitrary", "arbitrary")),
        name="ffn_up",
    )(h2, wg, wu, conv_w, w_down)


def kernel(x, c, norm1_g, w_ada, b_ada, w_in, conv_w, gla_a2, gla_a_bias, gla_norm_g,
           w_branch_a, w_branch_b, w_branch_c, w_out, norm2_g, w_ffn_gate, w_ffn_up,
           ffn_conv_w, w_ffn_down, final_norm_g):
    batch = x.shape[0]
    n_tok = batch * SEQ
    slopes = alibi_slopes(MOBA_HEADS)
    c_pad = jnp.zeros((8, D_MODEL), F32).at[:batch].set(c)
    mod = ada_modulation(c_pad, w_ada, b_ada)
    mod = mod[:, :batch].reshape(DEPTH, batch, 6, 1, D_MODEL)
    w_in_t = jnp.swapaxes(w_in, 1, 2)

    for l in range(DEPTH):
        sh1, sc1, gt1, sh2, sc2, gt2 = (mod[l, :, t] for t in range(6))

        a2_pad = jnp.zeros((LANES, GLA_KW), F32).at[:GLA_RANK].set(gla_a2[l])

        h = norm_modulate(x, norm1_g[l], sc1, sh1)
        h2 = h.reshape(n_tok, D_MODEL)
        z, w_gl = in_proj_main(h2, w_in_t, l)
        ac = in_proj(h2, w_in_t, l, OFF_AC, LANES, F32, 2048, LANES, "in_proj_rank")
        z3 = z.reshape(batch, SEQ, Z_WIDTH)

        ya = moba_attention(z3, slopes)
        yb = gated_conv(z3, conv_w[l])
        yc = gla_mixer(z3, ac.reshape(batch, SEQ, LANES), a2_pad, gla_a_bias[l], gla_norm_g[l])

        merged, w_out_b = merge_branches(
            h2, ya.reshape(n_tok, MOBA_WIDTH), yb.reshape(n_tok, CONV_WIDTH),
            yc.reshape(n_tok, GLA_VW), w_gl, w_branch_a, w_branch_b, w_branch_c, w_out, l)
        x2 = proj_residual(merged, w_out_b, x.reshape(n_tok, D_MODEL), gt1, 2048, 512, "out_proj")
        x = x2.reshape(batch, SEQ, D_MODEL)

        h = norm_modulate(x, norm2_g[l], sc2, sh2)
        act, w_down_b = ffn_up(h.reshape(n_tok, D_MODEL), w_ffn_gate, w_ffn_up, ffn_conv_w,
                               w_ffn_down, l)
        x2 = proj_residual(act, w_down_b, x2, gt2, 1024, 256, "ffn_down")
        x = x2.reshape(batch, SEQ, D_MODEL)

    return final_norm(x, final_norm_g)
```

```python
import functools
import math

import numpy as np
import jax
import jax.numpy as jnp
from jax import lax
from jax.experimental import pallas as pl
from jax.experimental.pallas import tpu as pltpu

F32 = jnp.float32
BF16 = jnp.bfloat16
F8 = jnp.float8_e4m3fn
F8_ROW_MAX = 256.0

D_MODEL = 4096
SEQ = 2048
DEPTH = 2

HEAD_DIM = 128
MOBA_HEADS = 12
MOBA_WIDTH = MOBA_HEADS * HEAD_DIM
MOBA_BLOCK = 256
MOBA_TOPK = 3
N_MOBA_BLOCKS = SEQ // MOBA_BLOCK
CONV_WIDTH = 1024
GLA_HEADS = 12
GLA_DK = 64
GLA_DV = 128
GLA_KW = GLA_HEADS * GLA_DK
GLA_VW = GLA_HEADS * GLA_DV
GLA_RANK = 16
GLA_TAU = 16.0
GLA_CHUNK = 64
D_FF = 11008
NORM_EPS = 1e-6

OFF_QA = 0
OFF_KA = OFF_QA + MOBA_WIDTH
OFF_VA = OFF_KA + MOBA_WIDTH
OFF_BX = OFF_VA + MOBA_WIDTH
OFF_BG = OFF_BX + CONV_WIDTH
OFF_CG = OFF_BG + CONV_WIDTH
OFF_QC = OFF_CG + CONV_WIDTH
OFF_KC = OFF_QC + GLA_KW
OFF_VC = OFF_KC + GLA_KW
OFF_RC = OFF_VC + GLA_VW
OFF_AC = OFF_RC + GLA_VW
OFF_GATE = OFF_AC + GLA_RANK
Z_WIDTH = OFF_AC

LANES = 128
VMEM_LIMIT = 56 * 1024 * 1024
RESIDENT = pl.Buffered(1)


def _params(sem):
    return pltpu.CompilerParams(dimension_semantics=sem, vmem_limit_bytes=VMEM_LIMIT)


def _dot(a, b):
    return jnp.dot(a, b, preferred_element_type=F32)


def _dot_nt(a, b):
    return lax.dot_general(a, b, (((1,), (1,)), ((), ())), preferred_element_type=F32)


def _dot_tn(a, b):
    return lax.dot_general(a, b, (((0,), (0,)), ((), ())), preferred_element_type=F32)


def _split_bf16(x, parts):
    out = []
    r = x
    for _ in range(parts):
        p = r.astype(BF16)
        out.append(p)
        r = r - p.astype(F32)
    return out


def _silu(x):
    return x * jax.nn.sigmoid(x)


def _quantize_rows(v):
    amax = jnp.max(jnp.abs(v), axis=-1, keepdims=True)
    q = (v * (F8_ROW_MAX / jnp.maximum(amax, 1e-30))).astype(F8)
    return q, amax * (1.0 / F8_ROW_MAX)


def alibi_slopes(n):
    def pow2(m):
        start = 2.0 ** (-(2.0 ** -(math.log2(m) - 3)))
        return [start ** (i + 1) for i in range(m)]
    if math.log2(n).is_integer():
        s = pow2(n)
    else:
        closest = 2 ** math.floor(math.log2(n))
        s = pow2(closest) + pow2(2 * closest)[0::2][: n - closest]
    return jnp.asarray(np.array(s, dtype=np.float32))


def _ada_kernel(c_ref, w_ref, b_ref, o_ref):
    ca = _silu(c_ref[...]).astype(BF16)
    o_ref[0] = _dot(ca, w_ref[0].astype(BF16)) + b_ref[0]


def ada_modulation(c_pad, w_ada, b_ada):
    n_out = w_ada.shape[-1]
    tn = 512
    return pl.pallas_call(
        _ada_kernel,
        grid=(DEPTH, n_out // tn),
        in_specs=[
            pl.BlockSpec((8, D_MODEL), lambda l, j: (0, 0)),
            pl.BlockSpec((1, D_MODEL, tn), lambda l, j: (l, 0, j)),
            pl.BlockSpec((1, 1, tn), lambda l, j: (l, 0, j)),
        ],
        out_specs=pl.BlockSpec((1, 8, tn), lambda l, j: (l, 0, j)),
        out_shape=jax.ShapeDtypeStruct((DEPTH, 8, n_out), F32),
        compiler_params=_params(("arbitrary", "arbitrary")),
        name="ada_modulation",
    )(c_pad, w_ada, b_ada.reshape(DEPTH, 1, n_out))


def _norm_mod_kernel(x_ref, g_ref, sc_ref, sh_ref, o_ref):
    x = x_ref[0]
    y = x * lax.rsqrt(jnp.mean(x * x, axis=-1, keepdims=True) + NORM_EPS) * g_ref[...]
    o_ref[0] = (y * (1.0 + sc_ref[0]) + sh_ref[0]).astype(o_ref.dtype)


def _norm_mod_q_kernel(x_ref, g_ref, sc_ref, sh_ref, o_ref, q_ref, dq_ref):
    x = x_ref[0]
    y = x * lax.rsqrt(jnp.mean(x * x, axis=-1, keepdims=True) + NORM_EPS) * g_ref[...]
    h = y * (1.0 + sc_ref[0]) + sh_ref[0]
    o_ref[0] = h.astype(o_ref.dtype)
    q_ref[0], dq_ref[0] = _quantize_rows(h)


def _norm_kernel(x_ref, g_ref, o_ref):
    x = x_ref[0]
    y = x * lax.rsqrt(jnp.mean(x * x, axis=-1, keepdims=True) + NORM_EPS) * g_ref[...]
    o_ref[0] = y.astype(o_ref.dtype)


def norm_modulate(x, g, sc, sh, with_fp8=False):
    tm = 256
    batch = x.shape[0]
    tile = pl.BlockSpec((1, tm, D_MODEL), lambda b, i: (b, i, 0))
    out_specs, out_shape = tile, jax.ShapeDtypeStruct((batch, SEQ, D_MODEL), BF16)
    if with_fp8:
        out_specs = [tile, tile, pl.BlockSpec((1, tm, 1), lambda b, i: (b, i, 0))]
        out_shape = [out_shape, jax.ShapeDtypeStruct((batch, SEQ, D_MODEL), F8),
                     jax.ShapeDtypeStruct((batch, SEQ, 1), F32)]
    return pl.pallas_call(
        _norm_mod_q_kernel if with_fp8 else _norm_mod_kernel,
        grid=(batch, SEQ // tm),
        in_specs=[
            tile,
            pl.BlockSpec((1, D_MODEL), lambda b, i: (0, 0)),
            pl.BlockSpec((1, 1, D_MODEL), lambda b, i: (b, 0, 0)),
            pl.BlockSpec((1, 1, D_MODEL), lambda b, i: (b, 0, 0)),
        ],
        out_specs=out_specs,
        out_shape=out_shape,
        compiler_params=_params(("arbitrary", "arbitrary")),
        name="norm_modulate",
    )(x, g.reshape(1, D_MODEL), sc, sh)


def final_norm(x, g):
    tm = 256
    batch = x.shape[0]
    return pl.pallas_call(
        _norm_kernel,
        grid=(batch, SEQ // tm),
        in_specs=[
            pl.BlockSpec((1, tm, D_MODEL), lambda b, i: (b, i, 0)),
            pl.BlockSpec((1, D_MODEL), lambda b, i: (0, 0)),
        ],
        out_specs=pl.BlockSpec((1, tm, D_MODEL), lambda b, i: (b, i, 0)),
        out_shape=jax.ShapeDtypeStruct((batch, SEQ, D_MODEL), F32),
        compiler_params=_params(("arbitrary", "arbitrary")),
        name="final_norm",
    )(x, g.reshape(1, D_MODEL))


def _mm_kernel(a_ref, w_ref, o_ref):
    o_ref[...] = _dot_nt(a_ref[...], w_ref[...].astype(BF16)).astype(o_ref.dtype)


def in_proj(a, w_in_t, layer, col0, n, out_dtype, tm, tn, name):
    m, k = a.shape
    j0 = col0 // tn
    return pl.pallas_call(
        _mm_kernel,
        grid=(m // tm, n // tn),
        in_specs=[
            pl.BlockSpec((tm, k), lambda i, j: (i, 0), pipeline_mode=RESIDENT),
            pl.BlockSpec((None, tn, k), lambda i, j: (layer, j0 + j, 0)),
        ],
        out_specs=pl.BlockSpec((tm, tn), lambda i, j: (i, j)),
        out_shape=jax.ShapeDtypeStruct((m, n), out_dtype),
        compiler_params=_params(("arbitrary", "arbitrary")),
        name=name,
    )(a, w_in_t)


def _in_proj_main_kernel(a_ref, w_ref, gw_ref, o_ref, gwq_ref, gdq_ref):
    o_ref[...] = _dot_nt(a_ref[...], w_ref[...].astype(BF16)).astype(o_ref.dtype)
    q, dq = _quantize_rows(gw_ref[0])
    gwq_ref[...] = q
    n = dq.shape[0]
    r = lax.broadcasted_iota(jnp.int32, (n, n), 0)
    c = lax.broadcasted_iota(jnp.int32, (n, n), 1)
    gdq_ref[...] = jnp.sum(jnp.where(r == c, dq, 0.0), axis=0, keepdims=True)


def in_proj_main(a, w_in_t, layer):
    tm, tn = 2048, 512
    m, k = a.shape
    ni, nj = m // tm, Z_WIDTH // tn
    n_gate = 3 * D_MODEL
    slab = n_gate // (ni * nj)
    assert slab * ni * nj == n_gate and slab % LANES == 0
    return pl.pallas_call(
        _in_proj_main_kernel,
        grid=(ni, nj),
        in_specs=[
            pl.BlockSpec((tm, k), lambda i, j: (i, 0), pipeline_mode=RESIDENT),
            pl.BlockSpec((None, tn, k), lambda i, j: (layer, j, 0)),
            pl.BlockSpec((pl.Element(1), pl.Element(slab), pl.Element(k)),
                         lambda i, j: (layer, pl.multiple_of(OFF_GATE + (i * nj + j) * slab, GLA_RANK), 0)),
        ],
        out_specs=[
            pl.BlockSpec((tm, tn), lambda i, j: (i, j)),
            pl.BlockSpec((slab, k), lambda i, j: (i * nj + j, 0)),
            pl.BlockSpec((1, slab), lambda i, j: (0, i * nj + j)),
        ],
        out_shape=[
            jax.ShapeDtypeStruct((m, Z_WIDTH), BF16),
            jax.ShapeDtypeStruct((n_gate, k), F8),
            jax.ShapeDtypeStruct((1, n_gate), F32),
        ],
        compiler_params=_params(("arbitrary", "arbitrary")),
        name="in_proj",
    )(a, w_in_t, w_in_t)


def _moba_kernel(slopes_ref, q_ref, k_ref, v_ref, o_ref, s_ref):
    slope = slopes_ref[pl.program_id(1)]
    scale = HEAD_DIM ** -0.5
    blk = MOBA_BLOCK

    r = lax.broadcasted_iota(jnp.int32, (LANES, SEQ), 0)
    t = lax.broadcasted_iota(jnp.int32, (LANES, SEQ), 1)
    pm = jnp.where(lax.shift_right_logical(t, 8) == r, 1.0 / blk, 0.0).astype(BF16)
    kmh, kml = _split_bf16(_dot(pm, k_ref[0]), 2)

    row = lax.broadcasted_iota(jnp.int32, (blk, blk), 0)
    col = lax.broadcasted_iota(jnp.int32, (blk, blk), 1)
    srel = slope * (col - row).astype(F32)
    causal = col <= row
    eye = jnp.where(row == col, 1.0, 0.0).astype(BF16)
    blk_id = lax.broadcasted_iota(jnp.int32, (N_MOBA_BLOCKS, blk), 0)

    for qi in range(N_MOBA_BLOCKS):
        rows = slice(qi * blk, (qi + 1) * blk)
        q_raw = q_ref[0, rows, :]
        qs = (q_raw.astype(F32) * scale).astype(BF16)

        sel_b = None
        if qi > MOBA_TOPK:
            g = (_dot_nt(kmh, q_raw) + _dot_nt(kml, q_raw))[:N_MOBA_BLOCKS] * scale
            rank = jnp.zeros_like(g)
            for m in range(qi):
                gm = g[m:m + 1, :]
                tie = jnp.where(blk_id > m, 1.0, 0.0)
                rank = rank + jnp.where(gm > g, 1.0, jnp.where(gm == g, tie, 0.0))
            sel = jnp.where(rank < float(MOBA_TOPK), 1.0, 0.0)
            sel_b = [_dot_nt(eye, jnp.broadcast_to(sel[n:n + 1, :], (LANES, blk)).astype(BF16))
                     for n in range(qi)]

        m_run = None
        for n in range(qi + 1):
            cols = slice(n * blk, (n + 1) * blk)
            s = _dot_nt(qs, k_ref[0, cols, :]) + srel
            if n == qi:
                s = jnp.where(causal, s, -jnp.inf)
            s_ref[:, cols] = s
            rm = jnp.max(s, axis=1, keepdims=True) - slope * float(blk * (qi - n))
            if sel_b is not None and n < qi:
                rm = jnp.where(sel_b[n][:, 0:1] > 0.5, rm, -jnp.inf)
            m_run = rm if m_run is None else jnp.maximum(m_run, rm)

        l_run = None
        acc = None
        for n in range(qi + 1):
            cols = slice(n * blk, (n + 1) * blk)
            p = jnp.exp(s_ref[:, cols] - (m_run + slope * float(blk * (qi - n))))
            rs = jnp.sum(p, axis=1, keepdims=True)
            pv = _dot(p.astype(BF16), v_ref[0, cols, :])
            if sel_b is not None and n < qi:
                rs = jnp.where(sel_b[n][:, 0:1] > 0.5, rs, 0.0)
                pv = jnp.where(sel_b[n] > 0.5, pv, 0.0)
            l_run = rs if l_run is None else l_run + rs
            acc = pv if acc is None else acc + pv
        o_ref[0, rows, :] = (acc / l_run).astype(o_ref.dtype)


def moba_attention(z3, slopes):
    batch = z3.shape[0]
    cq, ck, cv = OFF_QA // HEAD_DIM, OFF_KA // HEAD_DIM, OFF_VA // HEAD_DIM
    return pl.pallas_call(
        _moba_kernel,
        grid=(batch, MOBA_HEADS),
        in_specs=[
            pl.BlockSpec(memory_space=pltpu.SMEM),
            pl.BlockSpec((1, SEQ, HEAD_DIM), lambda b, h: (b, 0, cq + h)),
            pl.BlockSpec((1, SEQ, HEAD_DIM), lambda b, h: (b, 0, ck + h)),
            pl.BlockSpec((1, SEQ, HEAD_DIM), lambda b, h: (b, 0, cv + h)),
        ],
        out_specs=pl.BlockSpec((1, SEQ, HEAD_DIM), lambda b, h: (b, 0, h)),
        out_shape=jax.ShapeDtypeStruct((batch, SEQ, MOBA_WIDTH), BF16),
        scratch_shapes=[pltpu.VMEM((MOBA_BLOCK, SEQ), F32)],
        compiler_params=_params(("arbitrary", "arbitrary")),
        name="moba_attention",
    )(slopes, z3, z3, z3)


def _shift_rows(x, k):
    rolled = pltpu.roll(x, k, 0)
    row = lax.broadcasted_iota(jnp.int32, x.shape, 0)
    return jnp.where(row < k, 0.0, rolled)


def _gconv_kernel(bx_ref, bg_ref, cg_ref, w_ref, o_ref):
    xin = cg_ref[0].astype(F32) * bx_ref[0].astype(F32)
    w = w_ref[...]
    y = w[2:3] * xin + w[1:2] * _shift_rows(xin, 1) + w[0:1] * _shift_rows(xin, 2)
    o_ref[0] = (bg_ref[0].astype(F32) * y).astype(o_ref.dtype)


def gated_conv(z3, conv_w):
    tc = 256
    batch = z3.shape[0]
    cx, cg_, cc = OFF_BX // tc, OFF_BG // tc, OFF_CG // tc
    return pl.pallas_call(
        _gconv_kernel,
        grid=(batch, CONV_WIDTH // tc),
        in_specs=[
            pl.BlockSpec((1, SEQ, tc), lambda b, j: (b, 0, cx + j)),
            pl.BlockSpec((1, SEQ, tc), lambda b, j: (b, 0, cg_ + j)),
            pl.BlockSpec((1, SEQ, tc), lambda b, j: (b, 0, cc + j)),
            pl.BlockSpec((3, tc), lambda b, j: (0, j)),
        ],
        out_specs=pl.BlockSpec((1, SEQ, tc), lambda b, j: (b, 0, j)),
        out_shape=jax.ShapeDtypeStruct((batch, SEQ, CONV_WIDTH), BF16),
        compiler_params=_params(("arbitrary", "arbitrary")),
        name="gated_conv",
    )(z3, z3, z3, conv_w)


GLA_GROUP = 256


def _gla_kernel(q_ref, k_ref, v_ref, r_ref, ac_ref, a2_ref, ab_ref, ng_ref, o_ref):
    cl, gr = GLA_CHUNK, GLA_GROUP

    lane = lax.broadcasted_iota(jnp.int32, (1, 2 * GLA_DK), 1)
    head0 = jnp.where(lane < GLA_DK, 1.0, 0.0)
    head1 = 1.0 - head0
    ri = lax.broadcasted_iota(jnp.int32, (gr, gr), 0)
    ci = lax.broadcasted_iota(jnp.int32, (gr, gr), 1)
    same_chunk = lax.shift_right_logical(ri, 6) == lax.shift_right_logical(ci, 6)
    causal = jnp.logical_and(same_chunk, ci <= ri)
    tril = jnp.where(causal, 1.0, 0.0).astype(BF16)
    ones = jnp.where(same_chunk, 1.0, 0.0).astype(BF16)
    sr = lax.broadcasted_iota(jnp.int32, (2 * GLA_DV, 2 * GLA_DK), 0)
    sc = lax.broadcasted_iota(jnp.int32, (2 * GLA_DV, 2 * GLA_DK), 1)
    same_head = jnp.where((sr < GLA_DV) == (sc < GLA_DK), 1.0, 0.0)

    a2h, a2l = _split_bf16(a2_ref[...], 2)
    a_bias = ab_ref[...]
    ng = ng_ref[...]

    st = jnp.zeros((2 * GLA_DV, 2 * GLA_DK), F32)
    for grp in range(SEQ // gr):
        rows = slice(grp * gr, (grp + 1) * gr)
        ah, al = _split_bf16(ac_ref[0, rows, :], 2)
        zz = _dot(ah, a2h) + _dot(ah, a2l) + _dot(al, a2h) + a_bias
        g = (jnp.minimum(zz, 0.0) - jnp.log1p(jnp.exp(-jnp.abs(zz)))) * (1.0 / GLA_TAU)
        g0, g1, g2 = _split_bf16(g, 3)
        bcum = _dot(tril, g0) + _dot(tril, g1) + _dot(tril, g2)
        btot = _dot(ones, g0) + _dot(ones, g1) + _dot(ones, g2)
        qf = q_ref[0, rows, :].astype(F32) * (GLA_DK ** -0.5)
        kf = k_ref[0, rows, :].astype(F32)
        q_i = qf * jnp.exp(bcum)
        k_i = (kf * jnp.exp(-bcum)).astype(BF16)
        k_dec = (kf * jnp.exp(btot - bcum)).astype(BF16)
        dec = jnp.exp(btot)
        v = v_ref[0, rows, :]

        halves = []
        for hh, hm in enumerate((head0, head1)):
            att = _dot_nt((q_i * hm).astype(BF16), k_i)
            att = jnp.where(causal, att, 0.0).astype(BF16)
            halves.append(_dot(att, v[:, hh * GLA_DV:(hh + 1) * GLA_DV]))
        o_intra = jnp.concatenate(halves, axis=1)

        q_b = q_i.astype(BF16)
        inter = []
        for c in range(gr // cl):
            cr = slice(c * cl, (c + 1) * cl)
            inter.append(_dot_nt(q_b[cr], st.astype(BF16)))
            st = st * dec[c * cl:c * cl + 1, :] + _dot_tn(v[cr], k_dec[cr]) * same_head
        o = o_intra + jnp.concatenate(inter, axis=0)

        outs = []
        for hh in range(2):
            oh = o[:, hh * GLA_DV:(hh + 1) * GLA_DV]
            ms = jnp.mean(oh * oh, axis=-1, keepdims=True)
            outs.append(oh * lax.rsqrt(ms + NORM_EPS) * ng)
        y = jnp.concatenate(outs, axis=1)
        o_ref[0, rows, :] = (_silu(r_ref[0, rows, :].astype(F32)) * y).astype(o_ref.dtype)


def gla_mixer(z3, ac3, a2_pad, a_bias, norm_g):
    batch = z3.shape[0]
    kw, vw = 2 * GLA_DK, 2 * GLA_DV
    cq, ck, cv, cr = OFF_QC // kw, OFF_KC // kw, OFF_VC // vw, OFF_RC // vw
    return pl.pallas_call(
        _gla_kernel,
        grid=(batch, GLA_HEADS // 2),
        in_specs=[
            pl.BlockSpec((1, SEQ, kw), lambda b, j: (b, 0, cq + j)),
            pl.BlockSpec((1, SEQ, kw), lambda b, j: (b, 0, ck + j)),
            pl.BlockSpec((1, SEQ, vw), lambda b, j: (b, 0, cv + j)),
            pl.BlockSpec((1, SEQ, vw), lambda b, j: (b, 0, cr + j)),
            pl.BlockSpec((1, SEQ, LANES), lambda b, j: (b, 0, 0)),
            pl.BlockSpec((LANES, kw), lambda b, j: (0, j)),
            pl.BlockSpec((1, kw), lambda b, j: (0, j)),
            pl.BlockSpec((1, GLA_DV), lambda b, j: (0, 0)),
        ],
        out_specs=pl.BlockSpec((1, SEQ, vw), lambda b, j: (b, 0, j)),
        out_shape=jax.ShapeDtypeStruct((batch, SEQ, GLA_VW), BF16),
        compiler_params=_params(("arbitrary", "arbitrary")),
        name="gla_mixer",
    )(z3, z3, z3, z3, ac3, a2_pad, a_bias.reshape(1, GLA_KW), norm_g.reshape(1, GLA_DV))


def _merge_kernel(hq_ref, hdq_ref, ya_ref, yb_ref, yc_ref, wg0_ref, wg1_ref, wg2_ref,
                  dq0_ref, dq1_ref, dq2_ref, wa_ref, wb_ref, wc_ref, wo_ref, o_ref, wob_ref):
    wob_ref[...] = wo_ref[...].astype(wob_ref.dtype)
    hq = hq_ref[...]
    hdq = hdq_ref[...]

    def gate(wg_ref, dq_ref):
        return jax.nn.sigmoid(_dot_nt(hq, wg_ref[...]) * hdq * dq_ref[...])

    acc = gate(wg0_ref, dq0_ref) * _dot(ya_ref[...], wa_ref[...].astype(BF16))
    acc += gate(wg1_ref, dq1_ref) * _dot(yb_ref[...], wb_ref[...].astype(BF16))
    acc += gate(wg2_ref, dq2_ref) * _dot(yc_ref[...], wc_ref[...].astype(BF16))
    o_ref[...] = acc.astype(o_ref.dtype)


def merge_branches(hq, hdq, ya, yb, yc, w_gl, w_gl_dq, wa, wb, wc, w_out, layer):
    h2 = hq
    tm, tn = 1024, 256
    n_tok = h2.shape[0]
    ni, nj = n_tok // tm, D_MODEL // tn
    slab = D_MODEL // (ni * nj)
    assert slab * ni * nj == D_MODEL and slab % 16 == 0
    row = lambda i, j: (i, 0)
    colw = lambda i, j: (layer, 0, j)
    return pl.pallas_call(
        _merge_kernel,
        grid=(ni, nj),
        in_specs=[
            pl.BlockSpec((tm, D_MODEL), row, pipeline_mode=RESIDENT),
            pl.BlockSpec((tm, 1), row, pipeline_mode=RESIDENT),
            pl.BlockSpec((tm, MOBA_WIDTH), row, pipeline_mode=RESIDENT),
            pl.BlockSpec((tm, CONV_WIDTH), row, pipeline_mode=RESIDENT),
            pl.BlockSpec((tm, GLA_VW), row, pipeline_mode=RESIDENT),
            pl.BlockSpec((tn, D_MODEL), lambda i, j: (j, 0)),
            pl.BlockSpec((tn, D_MODEL), lambda i, j: (nj + j, 0)),
            pl.BlockSpec((tn, D_MODEL), lambda i, j: (2 * nj + j, 0)),
            pl.BlockSpec((1, tn), lambda i, j: (0, j)),
            pl.BlockSpec((1, tn), lambda i, j: (0, nj + j)),
            pl.BlockSpec((1, tn), lambda i, j: (0, 2 * nj + j)),
            pl.BlockSpec((None, MOBA_WIDTH, tn), colw),
            pl.BlockSpec((None, CONV_WIDTH, tn), colw),
            pl.BlockSpec((None, GLA_VW, tn), colw),
            pl.BlockSpec((None, slab, D_MODEL), lambda i, j: (layer, i * nj + j, 0)),
        ],
        out_specs=[
            pl.BlockSpec((tm, tn), lambda i, j: (i, j)),
            pl.BlockSpec((slab, D_MODEL), lambda i, j: (i * nj + j, 0)),
        ],
        out_shape=[
            jax.ShapeDtypeStruct((n_tok, D_MODEL), BF16),
            jax.ShapeDtypeStruct((D_MODEL, D_MODEL), BF16),
        ],
        compiler_params=_params(("arbitrary", "arbitrary")),
        name="merge_branches",
    )(hq, hdq, ya, yb, yc, w_gl, w_gl, w_gl, w_gl_dq, w_gl_dq, w_gl_dq, wa, wb, wc, w_out)


def _proj_residual_kernel(a_ref, w_ref, x_ref, gt_ref, o_ref):
    o_ref[...] = x_ref[...] + gt_ref[0] * _dot(a_ref[...], w_ref[...])


def proj_residual(a, w, x2, gt, tm, tn, name):
    m, k = a.shape
    n = w.shape[-1]
    per_batch = SEQ // tm
    return pl.pallas_call(
        _proj_residual_kernel,
        grid=(m // tm, n // tn),
        in_specs=[
            pl.BlockSpec((tm, k), lambda i, j: (i, 0), pipeline_mode=RESIDENT),
            pl.BlockSpec((k, tn), lambda i, j: (0, j)),
            pl.BlockSpec((tm, tn), lambda i, j: (i, j)),
            pl.BlockSpec((1, 1, tn), lambda i, j: (i // per_batch, 0, j)),
        ],
        out_specs=pl.BlockSpec((tm, tn), lambda i, j: (i, j)),
        out_shape=jax.ShapeDtypeStruct((m, n), F32),
        compiler_params=_params(("arbitrary", "arbitrary")),
        name=name,
    )(a, w, x2, gt)


def _ffn_up_kernel(a_ref, wg_ref, wu_ref, cw_ref, wd_ref, o_ref, wdb_ref, *, row_chunk):
    wdb_ref[...] = wd_ref[...].astype(wdb_ref.dtype)
    wg = wg_ref[...].astype(BF16)
    wu = wu_ref[...].astype(BF16)
    cw = cw_ref[...]
    prev = jnp.zeros((8, wg.shape[1]), F32)
    row = lax.broadcasted_iota(jnp.int32, (row_chunk, wg.shape[1]), 0)
    for r in range(a_ref.shape[0] // row_chunk):
        rows = slice(r * row_chunk, (r + 1) * row_chunk)
        a = a_ref[rows, :]
        g = _dot(a, wg)
        up = _dot(a, wu)
        s1 = jnp.where(row == 0, prev[7:8], pltpu.roll(g, 1, 0))
        s2 = jnp.where(row == 0, prev[6:7], jnp.where(row == 1, prev[7:8], pltpu.roll(g, 2, 0)))
        u = cw[2:3] * g + cw[1:2] * s1 + cw[0:1] * s2
        o_ref[rows, :] = (_silu(u) * up).astype(o_ref.dtype)
        prev = g[row_chunk - 8:row_chunk, :]


def ffn_up(h2, wg, wu, conv_w, w_down, layer):
    tm, tn = SEQ, 256
    n_tok = h2.shape[0]
    ni, nj = n_tok // tm, D_FF // tn
    slab = D_FF // (ni * nj)
    assert slab * ni * nj == D_FF and slab % 16 == 0
    return pl.pallas_call(
        functools.partial(_ffn_up_kernel, row_chunk=512),
        grid=(ni, nj),
        in_specs=[
            pl.BlockSpec((tm, D_MODEL), lambda i, j: (i, 0), pipeline_mode=RESIDENT),
            pl.BlockSpec((None, D_MODEL, tn), lambda i, j: (layer, 0, j)),
            pl.BlockSpec((None, D_MODEL, tn), lambda i, j: (layer, 0, j)),
            pl.BlockSpec((None, 3, tn), lambda i, j: (layer, 0, j)),
            pl.BlockSpec((None, slab, D_MODEL), lambda i, j: (layer, i * nj + j, 0)),
        ],
        out_specs=[
            pl.BlockSpec((tm, tn), lambda i, j: (i, j)),
            pl.BlockSpec((slab, D_MODEL), lambda i, j: (i * nj + j, 0)),
        ],
        out_shape=[
            jax.ShapeDtypeStruct((n_tok, D_FF), BF16),
            jax.ShapeDtypeStruct((D_FF, D_MODEL), BF16),
        ],
        compiler_params=_params(("arbitrary", "arbitrary")),
        name="ffn_up",
    )(h2, wg, wu, conv_w, w_down)


def kernel(x, c, norm1_g, w_ada, b_ada, w_in, conv_w, gla_a2, gla_a_bias, gla_norm_g,
           w_branch_a, w_branch_b, w_branch_c, w_out, norm2_g, w_ffn_gate, w_ffn_up,
           ffn_conv_w, w_ffn_down, final_norm_g):
    batch = x.shape[0]
    n_tok = batch * SEQ
    slopes = alibi_slopes(MOBA_HEADS)
    c_pad = jnp.zeros((8, D_MODEL), F32).at[:batch].set(c)
    mod = ada_modulation(c_pad, w_ada, b_ada)
    mod = mod[:, :batch].reshape(DEPTH, batch, 6, 1, D_MODEL)
    w_in_t = jnp.swapaxes(w_in, 1, 2)

    for l in range(DEPTH):
        sh1, sc1, gt1, sh2, sc2, gt2 = (mod[l, :, t] for t in range(6))

        a2_pad = jnp.zeros((LANES, GLA_KW), F32).at[:GLA_RANK].set(gla_a2[l])

        h, hq, hdq = norm_modulate(x, norm1_g[l], sc1, sh1, with_fp8=True)
        h2 = h.reshape(n_tok, D_MODEL)
        z, w_gl, w_gl_dq = in_proj_main(h2, w_in_t, l)
        ac = in_proj(h2, w_in_t, l, OFF_AC, LANES, F32, 2048, LANES, "in_proj_rank")
        z3 = z.reshape(batch, SEQ, Z_WIDTH)

        ya = moba_attention(z3, slopes)
        yb = gated_conv(z3, conv_w[l])
        yc = gla_mixer(z3, ac.reshape(batch, SEQ, LANES), a2_pad, gla_a_bias[l], gla_norm_g[l])

        merged, w_out_b = merge_branches(
            hq.reshape(n_tok, D_MODEL), hdq.reshape(n_tok, 1),
            ya.reshape(n_tok, MOBA_WIDTH), yb.reshape(n_tok, CONV_WIDTH), yc.reshape(n_tok, GLA_VW),
            w_gl, w_gl_dq, w_branch_a, w_branch_b, w_branch_c, w_out, l)
        x2 = proj_residual(merged, w_out_b, x.reshape(n_tok, D_MODEL), gt1, 2048, 512, "out_proj")
        x = x2.reshape(batch, SEQ, D_MODEL)

        h = norm_modulate(x, norm2_g[l], sc2, sh2)
        act, w_down_b = ffn_up(h.reshape(n_tok, D_MODEL), w_ffn_gate, w_ffn_up, ffn_conv_w,
                               w_ffn_down, l)
        x2 = proj_residual(act, w_down_b, x2, gt2, 1024, 256, "ffn_down")
        x = x2.reshape(batch, SEQ, D_MODEL)

    return final_norm(x, final_norm_g)
```

```python
import functools
import math

import numpy as np
import jax
import jax.numpy as jnp
from jax import lax
from jax.experimental import pallas as pl
from jax.experimental.pallas import tpu as pltpu

F32 = jnp.float32
BF16 = jnp.bfloat16
F8 = jnp.float8_e4m3fn
F8_ROW_MAX = 256.0

D_MODEL = 4096
SEQ = 2048
DEPTH = 2

HEAD_DIM = 128
MOBA_HEADS = 12
MOBA_WIDTH = MOBA_HEADS * HEAD_DIM
MOBA_BLOCK = 256
MOBA_TOPK = 3
N_MOBA_BLOCKS = SEQ // MOBA_BLOCK
CONV_WIDTH = 1024
GLA_HEADS = 12
GLA_DK = 64
GLA_DV = 128
GLA_KW = GLA_HEADS * GLA_DK
GLA_VW = GLA_HEADS * GLA_DV
GLA_RANK = 16
GLA_TAU = 16.0
GLA_CHUNK = 64
D_FF = 11008
NORM_EPS = 1e-6

OFF_QA = 0
OFF_KA = OFF_QA + MOBA_WIDTH
OFF_VA = OFF_KA + MOBA_WIDTH
OFF_BX = OFF_VA + MOBA_WIDTH
OFF_BG = OFF_BX + CONV_WIDTH
OFF_CG = OFF_BG + CONV_WIDTH
OFF_QC = OFF_CG + CONV_WIDTH
OFF_KC = OFF_QC + GLA_KW
OFF_VC = OFF_KC + GLA_KW
OFF_RC = OFF_VC + GLA_VW
OFF_AC = OFF_RC + GLA_VW
OFF_GATE = OFF_AC + GLA_RANK
Z_WIDTH = OFF_AC

LANES = 128
VMEM_LIMIT = 56 * 1024 * 1024
RESIDENT = pl.Buffered(1)


def _params(sem):
    return pltpu.CompilerParams(dimension_semantics=sem, vmem_limit_bytes=VMEM_LIMIT)


def _dot(a, b):
    return jnp.dot(a, b, preferred_element_type=F32)


def _dot_nt(a, b):
    return lax.dot_general(a, b, (((1,), (1,)), ((), ())), preferred_element_type=F32)


def _dot_tn(a, b):
    return lax.dot_general(a, b, (((0,), (0,)), ((), ())), preferred_element_type=F32)


def _split_bf16(x, parts):
    out = []
    r = x
    for _ in range(parts):
        p = r.astype(BF16)
        out.append(p)
        r = r - p.astype(F32)
    return out


def _silu(x):
    return x * jax.nn.sigmoid(x)


def _quantize_rows(v):
    amax = jnp.max(jnp.abs(v), axis=-1, keepdims=True)
    q = (v * (F8_ROW_MAX / jnp.maximum(amax, 1e-30))).astype(F8)
    return q, amax * (1.0 / F8_ROW_MAX)


def alibi_slopes(n):
    def pow2(m):
        start = 2.0 ** (-(2.0 ** -(math.log2(m) - 3)))
        return [start ** (i + 1) for i in range(m)]
    if math.log2(n).is_integer():
        s = pow2(n)
    else:
        closest = 2 ** math.floor(math.log2(n))
        s = pow2(closest) + pow2(2 * closest)[0::2][: n - closest]
    return jnp.asarray(np.array(s, dtype=np.float32))


def _ada_kernel(c_ref, w_ref, b_ref, o_ref):
    ca = _silu(c_ref[...]).astype(BF16)
    o_ref[0] = _dot(ca, w_ref[0].astype(BF16)) + b_ref[0]


def ada_modulation(c_pad, w_ada, b_ada):
    n_out = w_ada.shape[-1]
    tn = 512
    return pl.pallas_call(
        _ada_kernel,
        grid=(DEPTH, n_out // tn),
        in_specs=[
            pl.BlockSpec((8, D_MODEL), lambda l, j: (0, 0)),
            pl.BlockSpec((1, D_MODEL, tn), lambda l, j: (l, 0, j)),
            pl.BlockSpec((1, 1, tn), lambda l, j: (l, 0, j)),
        ],
        out_specs=pl.BlockSpec((1, 8, tn), lambda l, j: (l, 0, j)),
        out_shape=jax.ShapeDtypeStruct((DEPTH, 8, n_out), F32),
        compiler_params=_params(("arbitrary", "arbitrary")),
        name="ada_modulation",
    )(c_pad, w_ada, b_ada.reshape(DEPTH, 1, n_out))


def _norm_mod_kernel(x_ref, g_ref, sc_ref, sh_ref, o_ref):
    x = x_ref[0]
    y = x * lax.rsqrt(jnp.mean(x * x, axis=-1, keepdims=True) + NORM_EPS) * g_ref[...]
    o_ref[0] = (y * (1.0 + sc_ref[0]) + sh_ref[0]).astype(o_ref.dtype)


def _norm_mod_q_kernel(x_ref, g_ref, sc_ref, sh_ref, o_ref, q_ref, dq_ref):
    x = x_ref[0]
    y = x * lax.rsqrt(jnp.mean(x * x, axis=-1, keepdims=True) + NORM_EPS) * g_ref[...]
    h = y * (1.0 + sc_ref[0]) + sh_ref[0]
    o_ref[0] = h.astype(o_ref.dtype)
    q_ref[0], dq_ref[0] = _quantize_rows(h)


def _norm_kernel(x_ref, g_ref, o_ref):
    x = x_ref[0]
    y = x * lax.rsqrt(jnp.mean(x * x, axis=-1, keepdims=True) + NORM_EPS) * g_ref[...]
    o_ref[0] = y.astype(o_ref.dtype)


def norm_modulate(x, g, sc, sh, with_fp8=False):
    tm = 256
    batch = x.shape[0]
    tile = pl.BlockSpec((1, tm, D_MODEL), lambda b, i: (b, i, 0))
    out_specs, out_shape = tile, jax.ShapeDtypeStruct((batch, SEQ, D_MODEL), BF16)
    if with_fp8:
        out_specs = [tile, tile, pl.BlockSpec((1, tm, 1), lambda b, i: (b, i, 0))]
        out_shape = [out_shape, jax.ShapeDtypeStruct((batch, SEQ, D_MODEL), F8),
                     jax.ShapeDtypeStruct((batch, SEQ, 1), F32)]
    return pl.pallas_call(
        _norm_mod_q_kernel if with_fp8 else _norm_mod_kernel,
        grid=(batch, SEQ // tm),
        in_specs=[
            tile,
            pl.BlockSpec((1, D_MODEL), lambda b, i: (0, 0)),
            pl.BlockSpec((1, 1, D_MODEL), lambda b, i: (b, 0, 0)),
            pl.BlockSpec((1, 1, D_MODEL), lambda b, i: (b, 0, 0)),
        ],
        out_specs=out_specs,
        out_shape=out_shape,
        compiler_params=_params(("arbitrary", "arbitrary")),
        name="norm_modulate",
    )(x, g.reshape(1, D_MODEL), sc, sh)


def final_norm(x, g):
    tm = 256
    batch = x.shape[0]
    return pl.pallas_call(
        _norm_kernel,
        grid=(batch, SEQ // tm),
        in_specs=[
            pl.BlockSpec((1, tm, D_MODEL), lambda b, i: (b, i, 0)),
            pl.BlockSpec((1, D_MODEL), lambda b, i: (0, 0)),
        ],
        out_specs=pl.BlockSpec((1, tm, D_MODEL), lambda b, i: (b, i, 0)),
        out_shape=jax.ShapeDtypeStruct((batch, SEQ, D_MODEL), F32),
        compiler_params=_params(("arbitrary", "arbitrary")),
        name="final_norm",
    )(x, g.reshape(1, D_MODEL))


def _in_proj_main_kernel(a_ref, w_ref, wr_ref, gw_ref, o_ref, ac_ref, gwq_ref, gdq_ref):
    o_ref[...] = _dot_nt(a_ref[...], w_ref[...].astype(BF16)).astype(o_ref.dtype)

    @pl.when(pl.program_id(1) == 0)
    def _():
        ac_ref[...] = _dot_nt(a_ref[...], wr_ref[...].astype(BF16))

    q, dq = _quantize_rows(gw_ref[0])
    gwq_ref[...] = q
    n = dq.shape[0]
    r = lax.broadcasted_iota(jnp.int32, (n, n), 0)
    c = lax.broadcasted_iota(jnp.int32, (n, n), 1)
    gdq_ref[...] = jnp.sum(jnp.where(r == c, dq, 0.0), axis=0, keepdims=True)


def in_proj_main(a, w_in_t, layer):
    tm, tn = 2048, 512
    m, k = a.shape
    ni, nj = m // tm, Z_WIDTH // tn
    n_gate = 3 * D_MODEL
    slab = n_gate // (ni * nj)
    assert slab * ni * nj == n_gate and slab % LANES == 0
    return pl.pallas_call(
        _in_proj_main_kernel,
        grid=(ni, nj),
        in_specs=[
            pl.BlockSpec((tm, k), lambda i, j: (i, 0), pipeline_mode=RESIDENT),
            pl.BlockSpec((None, tn, k), lambda i, j: (layer, j, 0)),
            pl.BlockSpec((None, LANES, k), lambda i, j: (layer, OFF_AC // LANES, 0)),
            pl.BlockSpec((pl.Element(1), pl.Element(slab), pl.Element(k)),
                         lambda i, j: (layer, pl.multiple_of(OFF_GATE + (i * nj + j) * slab, GLA_RANK), 0)),
        ],
        out_specs=[
            pl.BlockSpec((tm, tn), lambda i, j: (i, j)),
            pl.BlockSpec((tm, LANES), lambda i, j: (i, 0)),
            pl.BlockSpec((slab, k), lambda i, j: (i * nj + j, 0)),
            pl.BlockSpec((1, slab), lambda i, j: (0, i * nj + j)),
        ],
        out_shape=[
            jax.ShapeDtypeStruct((m, Z_WIDTH), BF16),
            jax.ShapeDtypeStruct((m, LANES), F32),
            jax.ShapeDtypeStruct((n_gate, k), F8),
            jax.ShapeDtypeStruct((1, n_gate), F32),
        ],
        compiler_params=_params(("arbitrary", "arbitrary")),
        name="in_proj",
    )(a, w_in_t, w_in_t, w_in_t)


N_MOBA_PAIRS = N_MOBA_BLOCKS * (N_MOBA_BLOCKS + 1) // 2


def _moba_kernel(slopes_ref, q_ref, k_ref, v_ref, o_ref, s_ref, kt_ref, va_ref):
    slope = slopes_ref[pl.program_id(1)]
    scale = HEAD_DIM ** -0.5
    blk = MOBA_BLOCK
    nb = N_MOBA_BLOCKS

    r = lax.broadcasted_iota(jnp.int32, (LANES, SEQ), 0)
    t = lax.broadcasted_iota(jnp.int32, (LANES, SEQ), 1)
    pm = jnp.where(lax.shift_right_logical(t, 8) == r, 1.0 / blk, 0.0).astype(BF16)
    kmh, kml = _split_bf16(_dot(pm, k_ref[0]), 2)
    kt_ref[...] = k_ref[0].astype(F32).T.astype(BF16)
    va_ref[:, :HEAD_DIM] = v_ref[0]
    va_ref[:, HEAD_DIM:] = jnp.ones((SEQ, HEAD_DIM), BF16)

    row = lax.broadcasted_iota(jnp.int32, (blk, blk), 0)
    col = lax.broadcasted_iota(jnp.int32, (blk, blk), 1)
    srel = slope * (col - row).astype(F32)
    causal = col <= row
    eye = jnp.where(row == col, 1.0, 0.0).astype(BF16)
    blk_id = lax.broadcasted_iota(jnp.int32, (nb, blk), 0)
    pad = jnp.zeros((LANES - nb, blk), F32)
    pair = {(qi, n): qi * (qi + 1) // 2 + n for qi in range(nb) for n in range(qi + 1)}

    m_runs = []
    for qi in range(nb):
        q_raw = q_ref[0, qi * blk:(qi + 1) * blk, :]
        qs = (q_raw.astype(F32) * scale).astype(BF16)

        sel_t = None
        if qi > MOBA_TOPK:
            g = (_dot_nt(kmh, q_raw) + _dot_nt(kml, q_raw))[:nb] * scale
            rank = jnp.zeros_like(g)
            for m in range(qi):
                gm = g[m:m + 1, :]
                tie = jnp.where(blk_id > m, 1.0, 0.0)
                rank = rank + jnp.where(gm > g, 1.0, jnp.where(gm == g, tie, 0.0))
            sel = jnp.where(rank < float(MOBA_TOPK), 1.0, 0.0)
            sel_t = _dot_nt(eye, jnp.concatenate([sel, pad], axis=0).astype(BF16))

        m_run = None
        for n in range(qi + 1):
            s = _dot(qs, kt_ref[:, n * blk:(n + 1) * blk]) + srel
            if n == qi:
                s = jnp.where(causal, s, -jnp.inf)
            elif sel_t is not None:
                s = jnp.where(sel_t[:, n:n + 1] > 0.5, s, -jnp.inf)
            s_ref[pair[(qi, n)]] = s
            rm = jnp.max(s, axis=1, keepdims=True) - slope * float(blk * (qi - n))
            m_run = rm if m_run is None else jnp.maximum(m_run, rm)
        m_runs.append(m_run)

    for qi in range(nb):
        acc = None
        for n in range(qi + 1):
            p = jnp.exp(s_ref[pair[(qi, n)]] - (m_runs[qi] + slope * float(blk * (qi - n))))
            pv = _dot(p.astype(BF16), va_ref[n * blk:(n + 1) * blk, :])
            acc = pv if acc is None else acc + pv
        o_ref[0, qi * blk:(qi + 1) * blk, :] = (acc[:, :HEAD_DIM] / acc[:, HEAD_DIM:]).astype(o_ref.dtype)


def moba_attention(z3, slopes):
    batch = z3.shape[0]
    cq, ck, cv = OFF_QA // HEAD_DIM, OFF_KA // HEAD_DIM, OFF_VA // HEAD_DIM
    return pl.pallas_call(
        _moba_kernel,
        grid=(batch, MOBA_HEADS),
        in_specs=[
            pl.BlockSpec(memory_space=pltpu.SMEM),
            pl.BlockSpec((1, SEQ, HEAD_DIM), lambda b, h: (b, 0, cq + h)),
            pl.BlockSpec((1, SEQ, HEAD_DIM), lambda b, h: (b, 0, ck + h)),
            pl.BlockSpec((1, SEQ, HEAD_DIM), lambda b, h: (b, 0, cv + h)),
        ],
        out_specs=pl.BlockSpec((1, SEQ, HEAD_DIM), lambda b, h: (b, 0, h)),
        out_shape=jax.ShapeDtypeStruct((batch, SEQ, MOBA_WIDTH), BF16),
        scratch_shapes=[
            pltpu.VMEM((N_MOBA_PAIRS, MOBA_BLOCK, MOBA_BLOCK), F32),
            pltpu.VMEM((HEAD_DIM, SEQ), BF16),
            pltpu.VMEM((SEQ, 2 * HEAD_DIM), BF16),
        ],
        compiler_params=_params(("arbitrary", "arbitrary")),
        name="moba_attention",
    )(slopes, z3, z3, z3)


def _shift_rows(x, k):
    rolled = pltpu.roll(x, k, 0)
    row = lax.broadcasted_iota(jnp.int32, x.shape, 0)
    return jnp.where(row < k, 0.0, rolled)


def _gconv_kernel(bx_ref, bg_ref, cg_ref, w_ref, o_ref):
    xin = cg_ref[0].astype(F32) * bx_ref[0].astype(F32)
    w = w_ref[...]
    y = w[2:3] * xin + w[1:2] * _shift_rows(xin, 1) + w[0:1] * _shift_rows(xin, 2)
    o_ref[0] = (bg_ref[0].astype(F32) * y).astype(o_ref.dtype)


def gated_conv(z3, conv_w):
    tc = 256
    batch = z3.shape[0]
    cx, cg_, cc = OFF_BX // tc, OFF_BG // tc, OFF_CG // tc
    return pl.pallas_call(
        _gconv_kernel,
        grid=(batch, CONV_WIDTH // tc),
        in_specs=[
            pl.BlockSpec((1, SEQ, tc), lambda b, j: (b, 0, cx + j)),
            pl.BlockSpec((1, SEQ, tc), lambda b, j: (b, 0, cg_ + j)),
            pl.BlockSpec((1, SEQ, tc), lambda b, j: (b, 0, cc + j)),
            pl.BlockSpec((3, tc), lambda b, j: (0, j)),
        ],
        out_specs=pl.BlockSpec((1, SEQ, tc), lambda b, j: (b, 0, j)),
        out_shape=jax.ShapeDtypeStruct((batch, SEQ, CONV_WIDTH), BF16),
        compiler_params=_params(("arbitrary", "arbitrary")),
        name="gated_conv",
    )(z3, z3, z3, conv_w)


GLA_GROUP = 256


def _gla_kernel(q_ref, k_ref, v_ref, r_ref, ac_ref, a2_ref, ab_ref, ng_ref, o_ref):
    cl, gr = GLA_CHUNK, GLA_GROUP
    n_grp, per = SEQ // gr, gr // cl

    lane = lax.broadcasted_iota(jnp.int32, (1, 2 * GLA_DK), 1)
    head0 = jnp.where(lane < GLA_DK, 1.0, 0.0)
    heads = (head0, 1.0 - head0)
    ri = lax.broadcasted_iota(jnp.int32, (gr, gr), 0)
    ci = lax.broadcasted_iota(jnp.int32, (gr, gr), 1)
    same_chunk = lax.shift_right_logical(ri, 6) == lax.shift_right_logical(ci, 6)
    causal = jnp.logical_and(same_chunk, ci <= ri)
    tril = jnp.where(causal, 1.0, 0.0).astype(BF16)
    sr = lax.broadcasted_iota(jnp.int32, (2 * GLA_DV, 2 * GLA_DK), 0)
    sc = lax.broadcasted_iota(jnp.int32, (2 * GLA_DV, 2 * GLA_DK), 1)
    same_head = jnp.where((sr < GLA_DV) == (sc < GLA_DK), 1.0, 0.0)

    a2b = a2_ref[...].astype(BF16)
    a_bias = ab_ref[...]
    ng = ng_ref[...]
    grp_rows = [slice(i * gr, (i + 1) * gr) for i in range(n_grp)]

    zz = [_dot(ac_ref[0, rw, :].astype(BF16), a2b) + a_bias for rw in grp_rows]
    gl = [(jnp.minimum(z, 0.0) - jnp.log1p(jnp.exp(-jnp.abs(z)))) * (1.0 / GLA_TAU) for z in zz]
    bcum = []
    for g in gl:
        g0, g1 = _split_bf16(g, 2)
        bcum.append(_dot(tril, g0) + _dot(tril, g1))
    q_i, k_i, k_dec, dec = [], [], [], []
    for rw, bc in zip(grp_rows, bcum):
        btot = jnp.concatenate(
            [jnp.broadcast_to(bc[(c + 1) * cl - 1:(c + 1) * cl, :], (cl, 2 * GLA_DK))
             for c in range(per)], axis=0)
        qf = q_ref[0, rw, :].astype(F32) * (GLA_DK ** -0.5)
        kf = k_ref[0, rw, :].astype(F32)
        q_i.append(qf * jnp.exp(bc))
        k_i.append((kf * jnp.exp(-bc)).astype(BF16))
        k_dec.append((kf * jnp.exp(btot - bc)).astype(BF16))
        dec.append(jnp.exp(btot))
    o_intra = []
    for rw, qg, kg in zip(grp_rows, q_i, k_i):
        halves = []
        for hh, hm in enumerate(heads):
            att = _dot_nt((qg * hm).astype(BF16), kg)
            att = jnp.where(causal, att, 0.0).astype(BF16)
            halves.append(_dot(att, v_ref[0, rw, hh * GLA_DV:(hh + 1) * GLA_DV]))
        o_intra.append(jnp.concatenate(halves, axis=1))
    kv = []
    for i, rw in enumerate(grp_rows):
        v = v_ref[0, rw, :]
        kv.append([_dot_tn(v[c * cl:(c + 1) * cl], k_dec[i][c * cl:(c + 1) * cl]) * same_head
                   for c in range(per)])

    st = jnp.zeros((2 * GLA_DV, 2 * GLA_DK), F32)
    for i, rw in enumerate(grp_rows):
        qb = q_i[i].astype(BF16)
        inter = []
        for c in range(per):
            inter.append(_dot_nt(qb[c * cl:(c + 1) * cl], st.astype(BF16)))
            st = st * dec[i][c * cl:c * cl + 1, :] + kv[i][c]
        o = o_intra[i] + jnp.concatenate(inter, axis=0)
        outs = []
        for hh in range(2):
            oh = o[:, hh * GLA_DV:(hh + 1) * GLA_DV]
            ms = jnp.mean(oh * oh, axis=-1, keepdims=True)
            outs.append(oh * lax.rsqrt(ms + NORM_EPS) * ng)
        y = jnp.concatenate(outs, axis=1)
        o_ref[0, rw, :] = (_silu(r_ref[0, rw, :].astype(F32)) * y).astype(o_ref.dtype)


def gla_mixer(z3, ac3, a2_pad, a_bias, norm_g):
    batch = z3.shape[0]
    kw, vw = 2 * GLA_DK, 2 * GLA_DV
    cq, ck, cv, cr = OFF_QC // kw, OFF_KC // kw, OFF_VC // vw, OFF_RC // vw
    return pl.pallas_call(
        _gla_kernel,
        grid=(batch, GLA_HEADS // 2),
        in_specs=[
            pl.BlockSpec((1, SEQ, kw), lambda b, j: (b, 0, cq + j)),
            pl.BlockSpec((1, SEQ, kw), lambda b, j: (b, 0, ck + j)),
            pl.BlockSpec((1, SEQ, vw), lambda b, j: (b, 0, cv + j)),
            pl.BlockSpec((1, SEQ, vw), lambda b, j: (b, 0, cr + j)),
            pl.BlockSpec((1, SEQ, LANES), lambda b, j: (b, 0, 0)),
            pl.BlockSpec((LANES, kw), lambda b, j: (0, j)),
            pl.BlockSpec((1, kw), lambda b, j: (0, j)),
            pl.BlockSpec((1, GLA_DV), lambda b, j: (0, 0)),
        ],
        out_specs=pl.BlockSpec((1, SEQ, vw), lambda b, j: (b, 0, j)),
        out_shape=jax.ShapeDtypeStruct((batch, SEQ, GLA_VW), BF16),
        compiler_params=_params(("arbitrary", "arbitrary")),
        name="gla_mixer",
    )(z3, z3, z3, z3, ac3, a2_pad, a_bias.reshape(1, GLA_KW), norm_g.reshape(1, GLA_DV))


def _merge_kernel(hq_ref, hdq_ref, ya_ref, yb_ref, yc_ref, wg0_ref, wg1_ref, wg2_ref,
                  dq0_ref, dq1_ref, dq2_ref, wa_ref, wb_ref, wc_ref, wo_ref, o_ref, wob_ref):
    wob_ref[...] = wo_ref[...].astype(wob_ref.dtype)
    hq = hq_ref[...]
    hdq = hdq_ref[...]

    def gate(wg_ref, dq_ref):
        return jax.nn.sigmoid(_dot_nt(hq, wg_ref[...]) * hdq * dq_ref[...])

    acc = gate(wg0_ref, dq0_ref) * _dot(ya_ref[...], wa_ref[...].astype(BF16))
    acc += gate(wg1_ref, dq1_ref) * _dot(yb_ref[...], wb_ref[...].astype(BF16))
    acc += gate(wg2_ref, dq2_ref) * _dot(yc_ref[...], wc_ref[...].astype(BF16))
    o_ref[...] = acc.astype(o_ref.dtype)


def merge_branches(hq, hdq, ya, yb, yc, w_gl, w_gl_dq, wa, wb, wc, w_out, layer):
    h2 = hq
    tm, tn = 1024, 256
    n_tok = h2.shape[0]
    ni, nj = n_tok // tm, D_MODEL // tn
    slab = D_MODEL // (ni * nj)
    assert slab * ni * nj == D_MODEL and slab % 16 == 0
    row = lambda i, j: (i, 0)
    colw = lambda i, j: (layer, 0, j)
    return pl.pallas_call(
        _merge_kernel,
        grid=(ni, nj),
        in_specs=[
            pl.BlockSpec((tm, D_MODEL), row, pipeline_mode=RESIDENT),
            pl.BlockSpec((tm, 1), row, pipeline_mode=RESIDENT),
            pl.BlockSpec((tm, MOBA_WIDTH), row, pipeline_mode=RESIDENT),
            pl.BlockSpec((tm, CONV_WIDTH), row, pipeline_mode=RESIDENT),
            pl.BlockSpec((tm, GLA_VW), row, pipeline_mode=RESIDENT),
            pl.BlockSpec((tn, D_MODEL), lambda i, j: (j, 0)),
            pl.BlockSpec((tn, D_MODEL), lambda i, j: (nj + j, 0)),
            pl.BlockSpec((tn, D_MODEL), lambda i, j: (2 * nj + j, 0)),
            pl.BlockSpec((1, tn), lambda i, j: (0, j)),
            pl.BlockSpec((1, tn), lambda i, j: (0, nj + j)),
            pl.BlockSpec((1, tn), lambda i, j: (0, 2 * nj + j)),
            pl.BlockSpec((None, MOBA_WIDTH, tn), colw),
            pl.BlockSpec((None, CONV_WIDTH, tn), colw),
            pl.BlockSpec((None, GLA_VW, tn), colw),
            pl.BlockSpec((None, slab, D_MODEL), lambda i, j: (layer, i * nj + j, 0)),
        ],
        out_specs=[
            pl.BlockSpec((tm, tn), lambda i, j: (i, j)),
            pl.BlockSpec((slab, D_MODEL), lambda i, j: (i * nj + j, 0)),
        ],
        out_shape=[
            jax.ShapeDtypeStruct((n_tok, D_MODEL), BF16),
            jax.ShapeDtypeStruct((D_MODEL, D_MODEL), BF16),
        ],
        compiler_params=_params(("arbitrary", "arbitrary")),
        name="merge_branches",
    )(hq, hdq, ya, yb, yc, w_gl, w_gl, w_gl, w_gl_dq, w_gl_dq, w_gl_dq, wa, wb, wc, w_out)


def _proj_residual_kernel(a_ref, w_ref, x_ref, gt_ref, o_ref):
    o_ref[...] = x_ref[...] + gt_ref[0] * _dot(a_ref[...], w_ref[...])


def proj_residual(a, w, x2, gt, tm, tn, name):
    m, k = a.shape
    n = w.shape[-1]
    per_batch = SEQ // tm
    return pl.pallas_call(
        _proj_residual_kernel,
        grid=(m // tm, n // tn),
        in_specs=[
            pl.BlockSpec((tm, k), lambda i, j: (i, 0), pipeline_mode=RESIDENT),
            pl.BlockSpec((k, tn), lambda i, j: (0, j)),
            pl.BlockSpec((tm, tn), lambda i, j: (i, j)),
            pl.BlockSpec((1, 1, tn), lambda i, j: (i // per_batch, 0, j)),
        ],
        out_specs=pl.BlockSpec((tm, tn), lambda i, j: (i, j)),
        out_shape=jax.ShapeDtypeStruct((m, n), F32),
        compiler_params=_params(("arbitrary", "arbitrary")),
        name=name,
    )(a, w, x2, gt)


def _ffn_up_kernel(a_ref, wg_ref, wu_ref, cw_ref, wd_ref, o_ref, wdb_ref, *, row_chunk):
    wdb_ref[...] = wd_ref[...].astype(wdb_ref.dtype)
    wg = wg_ref[...].astype(BF16)
    wu = wu_ref[...].astype(BF16)
    cw = cw_ref[...]
    prev = jnp.zeros((8, wg.shape[1]), F32)
    row = lax.broadcasted_iota(jnp.int32, (row_chunk, wg.shape[1]), 0)
    for r in range(a_ref.shape[0] // row_chunk):
        rows = slice(r * row_chunk, (r + 1) * row_chunk)
        a = a_ref[rows, :]
        g = _dot(a, wg)
        up = _dot(a, wu)
        s1 = jnp.where(row == 0, prev[7:8], pltpu.roll(g, 1, 0))
        s2 = jnp.where(row == 0, prev[6:7], jnp.where(row == 1, prev[7:8], pltpu.roll(g, 2, 0)))
        u = cw[2:3] * g + cw[1:2] * s1 + cw[0:1] * s2
        o_ref[rows, :] = (_silu(u) * up).astype(o_ref.dtype)
        prev = g[row_chunk - 8:row_chunk, :]


def ffn_up(h2, wg, wu, conv_w, w_down, layer):
    tm, tn = SEQ, 256
    n_tok = h2.shape[0]
    ni, nj = n_tok // tm, D_FF // tn
    slab = D_FF // (ni * nj)
    assert slab * ni * nj == D_FF and slab % 16 == 0
    return pl.pallas_call(
        functools.partial(_ffn_up_kernel, row_chunk=512),
        grid=(ni, nj),
        in_specs=[
            pl.BlockSpec((tm, D_MODEL), lambda i, j: (i, 0), pipeline_mode=RESIDENT),
            pl.BlockSpec((None, D_MODEL, tn), lambda i, j: (layer, 0, j)),
            pl.BlockSpec((None, D_MODEL, tn), lambda i, j: (layer, 0, j)),
            pl.BlockSpec((None, 3, tn), lambda i, j: (layer, 0, j)),
            pl.BlockSpec((None, slab, D_MODEL), lambda i, j: (layer, i * nj + j, 0)),
        ],
        out_specs=[
            pl.BlockSpec((tm, tn), lambda i, j: (i, j)),
            pl.BlockSpec((slab, D_MODEL), lambda i, j: (i * nj + j, 0)),
        ],
        out_shape=[
            jax.ShapeDtypeStruct((n_tok, D_FF), BF16),
            jax.ShapeDtypeStruct((D_FF, D_MODEL), BF16),
        ],
        compiler_params=_params(("arbitrary", "arbitrary")),
        name="ffn_up",
    )(h2, wg, wu, conv_w, w_down)


def kernel(x, c, norm1_g, w_ada, b_ada, w_in, conv_w, gla_a2, gla_a_bias, gla_norm_g,
           w_branch_a, w_branch_b, w_branch_c, w_out, norm2_g, w_ffn_gate, w_ffn_up,
           ffn_conv_w, w_ffn_down, final_norm_g):
    batch = x.shape[0]
    n_tok = batch * SEQ
    slopes = alibi_slopes(MOBA_HEADS)
    c_pad = jnp.zeros((8, D_MODEL), F32).at[:batch].set(c)
    mod = ada_modulation(c_pad, w_ada, b_ada)
    mod = mod[:, :batch].reshape(DEPTH, batch, 6, 1, D_MODEL)
    w_in_t = jnp.swapaxes(w_in, 1, 2)

    for l in range(DEPTH):
        sh1, sc1, gt1, sh2, sc2, gt2 = (mod[l, :, t] for t in range(6))

        a2_pad = jnp.zeros((LANES, GLA_KW), F32).at[:GLA_RANK].set(gla_a2[l])

        h, hq, hdq = norm_modulate(x, norm1_g[l], sc1, sh1, with_fp8=True)
        h2 = h.reshape(n_tok, D_MODEL)
        z, ac, w_gl, w_gl_dq = in_proj_main(h2, w_in_t, l)
        z3 = z.reshape(batch, SEQ, Z_WIDTH)

        ya = moba_attention(z3, slopes)
        yb = gated_conv(z3, conv_w[l])
        yc = gla_mixer(z3, ac.reshape(batch, SEQ, LANES), a2_pad, gla_a_bias[l], gla_norm_g[l])

        merged, w_out_b = merge_branches(
            hq.reshape(n_tok, D_MODEL), hdq.reshape(n_tok, 1),
            ya.reshape(n_tok, MOBA_WIDTH), yb.reshape(n_tok, CONV_WIDTH), yc.reshape(n_tok, GLA_VW),
            w_gl, w_gl_dq, w_branch_a, w_branch_b, w_branch_c, w_out, l)
        x2 = proj_residual(merged, w_out_b, x.reshape(n_tok, D_MODEL), gt1, 2048, 512, "out_proj")
        x = x2.reshape(batch, SEQ, D_MODEL)

        h = norm_modulate(x, norm2_g[l], sc2, sh2)
        act, w_down_b = ffn_up(h.reshape(n_tok, D_MODEL), w_ffn_gate, w_ffn_up, ffn_conv_w,
                               w_ffn_down, l)
        x2 = proj_residual(act, w_down_b, x2, gt2, 1024, 256, "ffn_down")
        x = x2.reshape(batch, SEQ, D_MODEL)

    return final_norm(x, final_norm_g)
```

```python
import functools
import math

import numpy as np
import jax
import jax.numpy as jnp
from jax import lax
from jax.experimental import pallas as pl
from jax.experimental.pallas import tpu as pltpu

F32 = jnp.float32
BF16 = jnp.bfloat16
F8 = jnp.float8_e4m3fn
F8_ROW_MAX = 256.0

D_MODEL = 4096
SEQ = 2048
DEPTH = 2

HEAD_DIM = 128
MOBA_HEADS = 12
MOBA_WIDTH = MOBA_HEADS * HEAD_DIM
MOBA_BLOCK = 256
MOBA_TOPK = 3
N_MOBA_BLOCKS = SEQ // MOBA_BLOCK
CONV_WIDTH = 1024
GLA_HEADS = 12
GLA_DK = 64
GLA_DV = 128
GLA_KW = GLA_HEADS * GLA_DK
GLA_VW = GLA_HEADS * GLA_DV
GLA_RANK = 16
GLA_TAU = 16.0
GLA_CHUNK = 64
D_FF = 11008
NORM_EPS = 1e-6

OFF_QA = 0
OFF_KA = OFF_QA + MOBA_WIDTH
OFF_VA = OFF_KA + MOBA_WIDTH
OFF_BX = OFF_VA + MOBA_WIDTH
OFF_BG = OFF_BX + CONV_WIDTH
OFF_CG = OFF_BG + CONV_WIDTH
OFF_QC = OFF_CG + CONV_WIDTH
OFF_KC = OFF_QC + GLA_KW
OFF_VC = OFF_KC + GLA_KW
OFF_RC = OFF_VC + GLA_VW
OFF_AC = OFF_RC + GLA_VW
OFF_GATE = OFF_AC + GLA_RANK
Z_WIDTH = OFF_AC

LANES = 128
VMEM_LIMIT = 56 * 1024 * 1024
RESIDENT = pl.Buffered(1)


def _params(sem):
    return pltpu.CompilerParams(dimension_semantics=sem, vmem_limit_bytes=VMEM_LIMIT)


def _dot(a, b):
    return jnp.dot(a, b, preferred_element_type=F32)


def _dot_nt(a, b):
    return lax.dot_general(a, b, (((1,), (1,)), ((), ())), preferred_element_type=F32)


def _dot_tn(a, b):
    return lax.dot_general(a, b, (((0,), (0,)), ((), ())), preferred_element_type=F32)


def _split_bf16(x, parts):
    out = []
    r = x
    for _ in range(parts):
        p = r.astype(BF16)
        out.append(p)
        r = r - p.astype(F32)
    return out


def _silu(x):
    return x * jax.nn.sigmoid(x)


def _quantize_rows(v):
    amax = jnp.max(jnp.abs(v), axis=-1, keepdims=True)
    q = (v * (F8_ROW_MAX / jnp.maximum(amax, 1e-30))).astype(F8)
    return q, amax * (1.0 / F8_ROW_MAX)


def alibi_slopes(n):
    def pow2(m):
        start = 2.0 ** (-(2.0 ** -(math.log2(m) - 3)))
        return [start ** (i + 1) for i in range(m)]
    if math.log2(n).is_integer():
        s = pow2(n)
    else:
        closest = 2 ** math.floor(math.log2(n))
        s = pow2(closest) + pow2(2 * closest)[0::2][: n - closest]
    return jnp.asarray(np.array(s, dtype=np.float32))


def _ada_kernel(c_ref, w_ref, b_ref, o_ref):
    ca = _silu(c_ref[...]).astype(BF16)
    o_ref[0] = _dot(ca, w_ref[0].astype(BF16)) + b_ref[0]


def ada_modulation(c_pad, w_ada, b_ada):
    n_out = w_ada.shape[-1]
    tn = 512
    return pl.pallas_call(
        _ada_kernel,
        grid=(DEPTH, n_out // tn),
        in_specs=[
            pl.BlockSpec((8, D_MODEL), lambda l, j: (0, 0)),
            pl.BlockSpec((1, D_MODEL, tn), lambda l, j: (l, 0, j)),
            pl.BlockSpec((1, 1, tn), lambda l, j: (l, 0, j)),
        ],
        out_specs=pl.BlockSpec((1, 8, tn), lambda l, j: (l, 0, j)),
        out_shape=jax.ShapeDtypeStruct((DEPTH, 8, n_out), F32),
        compiler_params=_params(("arbitrary", "arbitrary")),
        name="ada_modulation",
    )(c_pad, w_ada, b_ada.reshape(DEPTH, 1, n_out))


def _norm_mod_kernel(x_ref, g_ref, sc_ref, sh_ref, o_ref):
    x = x_ref[0]
    y = x * lax.rsqrt(jnp.mean(x * x, axis=-1, keepdims=True) + NORM_EPS) * g_ref[...]
    o_ref[0] = (y * (1.0 + sc_ref[0]) + sh_ref[0]).astype(o_ref.dtype)


def _norm_mod_q_kernel(x_ref, g_ref, sc_ref, sh_ref, o_ref, q_ref, dq_ref):
    x = x_ref[0]
    y = x * lax.rsqrt(jnp.mean(x * x, axis=-1, keepdims=True) + NORM_EPS) * g_ref[...]
    h = y * (1.0 + sc_ref[0]) + sh_ref[0]
    o_ref[0] = h.astype(o_ref.dtype)
    q_ref[0], dq_ref[0] = _quantize_rows(h)


def _norm_kernel(x_ref, g_ref, o_ref):
    x = x_ref[0]
    y = x * lax.rsqrt(jnp.mean(x * x, axis=-1, keepdims=True) + NORM_EPS) * g_ref[...]
    o_ref[0] = y.astype(o_ref.dtype)


def norm_modulate(x, g, sc, sh, with_fp8=False):
    tm = 256
    batch = x.shape[0]
    tile = pl.BlockSpec((1, tm, D_MODEL), lambda b, i: (b, i, 0))
    out_specs, out_shape = tile, jax.ShapeDtypeStruct((batch, SEQ, D_MODEL), BF16)
    if with_fp8:
        out_specs = [tile, tile, pl.BlockSpec((1, tm, 1), lambda b, i: (b, i, 0))]
        out_shape = [out_shape, jax.ShapeDtypeStruct((batch, SEQ, D_MODEL), F8),
                     jax.ShapeDtypeStruct((batch, SEQ, 1), F32)]
    return pl.pallas_call(
        _norm_mod_q_kernel if with_fp8 else _norm_mod_kernel,
        grid=(batch, SEQ // tm),
        in_specs=[
            tile,
            pl.BlockSpec((1, D_MODEL), lambda b, i: (0, 0)),
            pl.BlockSpec((1, 1, D_MODEL), lambda b, i: (b, 0, 0)),
            pl.BlockSpec((1, 1, D_MODEL), lambda b, i: (b, 0, 0)),
        ],
        out_specs=out_specs,
        out_shape=out_shape,
        compiler_params=_params(("arbitrary", "arbitrary")),
        name="norm_modulate",
    )(x, g.reshape(1, D_MODEL), sc, sh)


def final_norm(x, g):
    tm = 256
    batch = x.shape[0]
    return pl.pallas_call(
        _norm_kernel,
        grid=(batch, SEQ // tm),
        in_specs=[
            pl.BlockSpec((1, tm, D_MODEL), lambda b, i: (b, i, 0)),
            pl.BlockSpec((1, D_MODEL), lambda b, i: (0, 0)),
        ],
        out_specs=pl.BlockSpec((1, tm, D_MODEL), lambda b, i: (b, i, 0)),
        out_shape=jax.ShapeDtypeStruct((batch, SEQ, D_MODEL), F32),
        compiler_params=_params(("arbitrary", "arbitrary")),
        name="final_norm",
    )(x, g.reshape(1, D_MODEL))


def _in_proj_main_kernel(a_ref, w_ref, wr_ref, gw_ref, o_ref, ac_ref, gwq_ref, gdq_ref):
    o_ref[...] = _dot_nt(a_ref[...], w_ref[...].astype(BF16)).astype(o_ref.dtype)

    @pl.when(pl.program_id(1) == 0)
    def _():
        ac_ref[...] = _dot_nt(a_ref[...], wr_ref[...].astype(BF16))

    q, dq = _quantize_rows(gw_ref[0])
    gwq_ref[...] = q
    n = dq.shape[0]
    r = lax.broadcasted_iota(jnp.int32, (n, n), 0)
    c = lax.broadcasted_iota(jnp.int32, (n, n), 1)
    gdq_ref[...] = jnp.sum(jnp.where(r == c, dq, 0.0), axis=0, keepdims=True)


def in_proj_main(a, w_in_t, layer):
    tm, tn = 2048, 512
    m, k = a.shape
    ni, nj = m // tm, Z_WIDTH // tn
    n_gate = 3 * D_MODEL
    slab = n_gate // (ni * nj)
    assert slab * ni * nj == n_gate and slab % LANES == 0
    return pl.pallas_call(
        _in_proj_main_kernel,
        grid=(ni, nj),
        in_specs=[
            pl.BlockSpec((tm, k), lambda i, j: (i, 0), pipeline_mode=RESIDENT),
            pl.BlockSpec((None, tn, k), lambda i, j: (layer, j, 0)),
            pl.BlockSpec((None, LANES, k), lambda i, j: (layer, OFF_AC // LANES, 0)),
            pl.BlockSpec((pl.Element(1), pl.Element(slab), pl.Element(k)),
                         lambda i, j: (layer, pl.multiple_of(OFF_GATE + (i * nj + j) * slab, GLA_RANK), 0)),
        ],
        out_specs=[
            pl.BlockSpec((tm, tn), lambda i, j: (i, j)),
            pl.BlockSpec((tm, LANES), lambda i, j: (i, 0)),
            pl.BlockSpec((slab, k), lambda i, j: (i * nj + j, 0)),
            pl.BlockSpec((1, slab), lambda i, j: (0, i * nj + j)),
        ],
        out_shape=[
            jax.ShapeDtypeStruct((m, Z_WIDTH), BF16),
            jax.ShapeDtypeStruct((m, LANES), F32),
            jax.ShapeDtypeStruct((n_gate, k), F8),
            jax.ShapeDtypeStruct((1, n_gate), F32),
        ],
        compiler_params=_params(("arbitrary", "arbitrary")),
        name="in_proj",
    )(a, w_in_t, w_in_t, w_in_t)


N_MOBA_PAIRS = N_MOBA_BLOCKS * (N_MOBA_BLOCKS + 1) // 2


def _moba_kernel(slopes_ref, q_ref, k_ref, v_ref, o_ref, s_ref, kt_ref, va_ref):
    slope = slopes_ref[pl.program_id(1)]
    scale = HEAD_DIM ** -0.5
    blk = MOBA_BLOCK
    nb = N_MOBA_BLOCKS

    r = lax.broadcasted_iota(jnp.int32, (LANES, SEQ), 0)
    t = lax.broadcasted_iota(jnp.int32, (LANES, SEQ), 1)
    pm = jnp.where(lax.shift_right_logical(t, 8) == r, 1.0 / blk, 0.0).astype(BF16)
    kmh, kml = _split_bf16(_dot(pm, k_ref[0]), 2)
    kt_ref[...] = k_ref[0].astype(F32).T.astype(BF16)
    va_ref[:, :HEAD_DIM] = v_ref[0]
    va_ref[:, HEAD_DIM:] = jnp.ones((SEQ, HEAD_DIM), BF16)

    row = lax.broadcasted_iota(jnp.int32, (blk, blk), 0)
    col = lax.broadcasted_iota(jnp.int32, (blk, blk), 1)
    srel = slope * (col - row).astype(F32)
    causal = col <= row
    eye = jnp.where(row == col, 1.0, 0.0).astype(BF16)
    blk_id = lax.broadcasted_iota(jnp.int32, (nb, blk), 0)
    pad = jnp.zeros((LANES - nb, blk), F32)
    pair = {(qi, n): qi * (qi + 1) // 2 + n for qi in range(nb) for n in range(qi + 1)}

    m_runs = []
    for qi in range(nb):
        q_raw = q_ref[0, qi * blk:(qi + 1) * blk, :]
        qs = (q_raw.astype(F32) * scale).astype(BF16)

        sel_t = None
        if qi > MOBA_TOPK:
            g = (_dot_nt(kmh, q_raw) + _dot_nt(kml, q_raw))[:nb] * scale
            rank = jnp.zeros_like(g)
            for m in range(qi):
                gm = g[m:m + 1, :]
                tie = jnp.where(blk_id > m, 1.0, 0.0)
                rank = rank + jnp.where(gm > g, 1.0, jnp.where(gm == g, tie, 0.0))
            sel = jnp.where(rank < float(MOBA_TOPK), 1.0, 0.0)
            sel_t = _dot_nt(eye, jnp.concatenate([sel, pad], axis=0).astype(BF16))

        m_run = None
        for n in range(qi + 1):
            s = _dot(qs, kt_ref[:, n * blk:(n + 1) * blk]) + srel
            if n == qi:
                s = jnp.where(causal, s, -jnp.inf)
            elif sel_t is not None:
                s = jnp.where(sel_t[:, n:n + 1] > 0.5, s, -jnp.inf)
            s_ref[pair[(qi, n)]] = s
            rm = jnp.max(s, axis=1, keepdims=True) - slope * float(blk * (qi - n))
            m_run = rm if m_run is None else jnp.maximum(m_run, rm)
        m_runs.append(m_run)

    for qi in range(nb):
        acc = None
        for n in range(qi + 1):
            p = jnp.exp(s_ref[pair[(qi, n)]] - (m_runs[qi] + slope * float(blk * (qi - n))))
            pv = _dot(p.astype(BF16), va_ref[n * blk:(n + 1) * blk, :])
            acc = pv if acc is None else acc + pv
        o_ref[0, qi * blk:(qi + 1) * blk, :] = (acc[:, :HEAD_DIM] / acc[:, HEAD_DIM:]).astype(o_ref.dtype)


def moba_attention(z3, slopes):
    batch = z3.shape[0]
    cq, ck, cv = OFF_QA // HEAD_DIM, OFF_KA // HEAD_DIM, OFF_VA // HEAD_DIM
    return pl.pallas_call(
        _moba_kernel,
        grid=(batch, MOBA_HEADS),
        in_specs=[
            pl.BlockSpec(memory_space=pltpu.SMEM),
            pl.BlockSpec((1, SEQ, HEAD_DIM), lambda b, h: (b, 0, cq + h)),
            pl.BlockSpec((1, SEQ, HEAD_DIM), lambda b, h: (b, 0, ck + h)),
            pl.BlockSpec((1, SEQ, HEAD_DIM), lambda b, h: (b, 0, cv + h)),
        ],
        out_specs=pl.BlockSpec((1, SEQ, HEAD_DIM), lambda b, h: (b, 0, h)),
        out_shape=jax.ShapeDtypeStruct((batch, SEQ, MOBA_WIDTH), BF16),
        scratch_shapes=[
            pltpu.VMEM((N_MOBA_PAIRS, MOBA_BLOCK, MOBA_BLOCK), F32),
            pltpu.VMEM((HEAD_DIM, SEQ), BF16),
            pltpu.VMEM((SEQ, 2 * HEAD_DIM), BF16),
        ],
        compiler_params=_params(("arbitrary", "arbitrary")),
        name="moba_attention",
    )(slopes, z3, z3, z3)


def _shift_rows(x, k):
    rolled = pltpu.roll(x, k, 0)
    row = lax.broadcasted_iota(jnp.int32, x.shape, 0)
    return jnp.where(row < k, 0.0, rolled)


def _gconv_kernel(bx_ref, bg_ref, cg_ref, w_ref, o_ref):
    xin = cg_ref[0].astype(F32) * bx_ref[0].astype(F32)
    w = w_ref[...]
    y = w[2:3] * xin + w[1:2] * _shift_rows(xin, 1) + w[0:1] * _shift_rows(xin, 2)
    o_ref[0] = (bg_ref[0].astype(F32) * y).astype(o_ref.dtype)


def gated_conv(z3, conv_w):
    tc = 256
    batch = z3.shape[0]
    cx, cg_, cc = OFF_BX // tc, OFF_BG // tc, OFF_CG // tc
    return pl.pallas_call(
        _gconv_kernel,
        grid=(batch, CONV_WIDTH // tc),
        in_specs=[
            pl.BlockSpec((1, SEQ, tc), lambda b, j: (b, 0, cx + j)),
            pl.BlockSpec((1, SEQ, tc), lambda b, j: (b, 0, cg_ + j)),
            pl.BlockSpec((1, SEQ, tc), lambda b, j: (b, 0, cc + j)),
            pl.BlockSpec((3, tc), lambda b, j: (0, j)),
        ],
        out_specs=pl.BlockSpec((1, SEQ, tc), lambda b, j: (b, 0, j)),
        out_shape=jax.ShapeDtypeStruct((batch, SEQ, CONV_WIDTH), BF16),
        compiler_params=_params(("arbitrary", "arbitrary")),
        name="gated_conv",
    )(z3, z3, z3, conv_w)


GLA_GROUP = 256


def _gla_kernel(q_ref, k_ref, v_ref, r_ref, ac_ref, a2_ref, ab_ref, ng_ref, o_ref):
    cl, gr = GLA_CHUNK, GLA_GROUP
    n_grp, per = SEQ // gr, gr // cl

    lane = lax.broadcasted_iota(jnp.int32, (1, 2 * GLA_DK), 1)
    head0 = jnp.where(lane < GLA_DK, 1.0, 0.0)
    heads = (head0, 1.0 - head0)
    ri = lax.broadcasted_iota(jnp.int32, (gr, gr), 0)
    ci = lax.broadcasted_iota(jnp.int32, (gr, gr), 1)
    same_chunk = lax.shift_right_logical(ri, 6) == lax.shift_right_logical(ci, 6)
    causal = jnp.logical_and(same_chunk, ci <= ri)
    tril = jnp.where(causal, 1.0, 0.0).astype(BF16)
    sr = lax.broadcasted_iota(jnp.int32, (2 * GLA_DV, 2 * GLA_DK), 0)
    sc = lax.broadcasted_iota(jnp.int32, (2 * GLA_DV, 2 * GLA_DK), 1)
    same_head = jnp.where((sr < GLA_DV) == (sc < GLA_DK), 1.0, 0.0)

    a2b = a2_ref[...].astype(BF16)
    a_bias = ab_ref[...]
    ng = ng_ref[...]
    grp_rows = [slice(i * gr, (i + 1) * gr) for i in range(n_grp)]

    zz = [_dot(ac_ref[0, rw, :].astype(BF16), a2b) + a_bias for rw in grp_rows]
    gl = [(jnp.minimum(z, 0.0) - jnp.log1p(jnp.exp(-jnp.abs(z)))) * (1.0 / GLA_TAU) for z in zz]
    bcum = []
    for g in gl:
        g0, g1 = _split_bf16(g, 2)
        bcum.append(_dot(tril, g0) + _dot(tril, g1))
    q_i, k_i, k_dec, dec = [], [], [], []
    for rw, bc in zip(grp_rows, bcum):
        btot = jnp.concatenate(
            [jnp.broadcast_to(bc[(c + 1) * cl - 1:(c + 1) * cl, :], (cl, 2 * GLA_DK))
             for c in range(per)], axis=0)
        qf = q_ref[0, rw, :].astype(F32) * (GLA_DK ** -0.5)
        kf = k_ref[0, rw, :].astype(F32)
        q_i.append(qf * jnp.exp(bc))
        k_i.append((kf * jnp.exp(-bc)).astype(BF16))
        k_dec.append((kf * jnp.exp(btot - bc)).astype(BF16))
        dec.append(jnp.exp(btot))
    o_intra = []
    for rw, qg, kg in zip(grp_rows, q_i, k_i):
        halves = []
        for hh, hm in enumerate(heads):
            att = _dot_nt((qg * hm).astype(BF16), kg)
            att = jnp.where(causal, att, 0.0).astype(BF16)
            halves.append(_dot(att, v_ref[0, rw, hh * GLA_DV:(hh + 1) * GLA_DV]))
        o_intra.append(jnp.concatenate(halves, axis=1))
    kv = []
    for i, rw in enumerate(grp_rows):
        v = v_ref[0, rw, :]
        kv.append([_dot_tn(v[c * cl:(c + 1) * cl], k_dec[i][c * cl:(c + 1) * cl]) * same_head
                   for c in range(per)])

    st = jnp.zeros((2 * GLA_DV, 2 * GLA_DK), F32)
    for i, rw in enumerate(grp_rows):
        qb = q_i[i].astype(BF16)
        inter = []
        for c in range(per):
            inter.append(_dot_nt(qb[c * cl:(c + 1) * cl], st.astype(BF16)))
            st = st * dec[i][c * cl:c * cl + 1, :] + kv[i][c]
        o = o_intra[i] + jnp.concatenate(inter, axis=0)
        outs = []
        for hh in range(2):
            oh = o[:, hh * GLA_DV:(hh + 1) * GLA_DV]
            ms = jnp.mean(oh * oh, axis=-1, keepdims=True)
            outs.append(oh * lax.rsqrt(ms + NORM_EPS) * ng)
        y = jnp.concatenate(outs, axis=1)
        o_ref[0, rw, :] = (_silu(r_ref[0, rw, :].astype(F32)) * y).astype(o_ref.dtype)


def gla_mixer(z3, ac3, a2_pad, a_bias, norm_g):
    batch = z3.shape[0]
    kw, vw = 2 * GLA_DK, 2 * GLA_DV
    cq, ck, cv, cr = OFF_QC // kw, OFF_KC // kw, OFF_VC // vw, OFF_RC // vw
    return pl.pallas_call(
        _gla_kernel,
        grid=(batch, GLA_HEADS // 2),
        in_specs=[
            pl.BlockSpec((1, SEQ, kw), lambda b, j: (b, 0, cq + j)),
            pl.BlockSpec((1, SEQ, kw), lambda b, j: (b, 0, ck + j)),
            pl.BlockSpec((1, SEQ, vw), lambda b, j: (b, 0, cv + j)),
            pl.BlockSpec((1, SEQ, vw), lambda b, j: (b, 0, cr + j)),
            pl.BlockSpec((1, SEQ, LANES), lambda b, j: (b, 0, 0)),
            pl.BlockSpec((LANES, kw), lambda b, j: (0, j)),
            pl.BlockSpec((1, kw), lambda b, j: (0, j)),
            pl.BlockSpec((1, GLA_DV), lambda b, j: (0, 0)),
        ],
        out_specs=pl.BlockSpec((1, SEQ, vw), lambda b, j: (b, 0, j)),
        out_shape=jax.ShapeDtypeStruct((batch, SEQ, GLA_VW), BF16),
        compiler_params=_params(("arbitrary", "arbitrary")),
        name="gla_mixer",
    )(z3, z3, z3, z3, ac3, a2_pad, a_bias.reshape(1, GLA_KW), norm_g.reshape(1, GLA_DV))


def _merge_kernel(hq_ref, hdq_ref, ya_ref, yb_ref, yc_ref, wg0_ref, wg1_ref, wg2_ref,
                  dq0_ref, dq1_ref, dq2_ref, wa_ref, wb_ref, wc_ref, wo_ref, o_ref, wob_ref,
                  *, row_chunk):
    wob_ref[...] = wo_ref[...].astype(wob_ref.dtype)
    wa = wa_ref[...].astype(BF16)
    wb = wb_ref[...].astype(BF16)
    wc = wc_ref[...].astype(BF16)
    for r in range(hq_ref.shape[0] // row_chunk):
        rows = slice(r * row_chunk, (r + 1) * row_chunk)
        hq = hq_ref[rows, :]
        hdq = hdq_ref[rows, :]

        def gate(wg_ref, dq_ref):
            return jax.nn.sigmoid(_dot_nt(hq, wg_ref[...]) * hdq * dq_ref[...])

        acc = gate(wg0_ref, dq0_ref) * _dot(ya_ref[rows, :], wa)
        acc += gate(wg1_ref, dq1_ref) * _dot(yb_ref[rows, :], wb)
        acc += gate(wg2_ref, dq2_ref) * _dot(yc_ref[rows, :], wc)
        o_ref[rows, :] = acc.astype(o_ref.dtype)


def merge_branches(hq, hdq, ya, yb, yc, w_gl, w_gl_dq, wa, wb, wc, w_out, layer):
    tm, tn = 1024, 512
    n_tok = hq.shape[0]
    ni, nj = n_tok // tm, D_MODEL // tn
    slab = D_MODEL // (ni * nj)
    assert slab * ni * nj == D_MODEL and slab % 16 == 0
    row = lambda i, j: (i, 0)
    colw = lambda i, j: (layer, 0, j)
    return pl.pallas_call(
        functools.partial(_merge_kernel, row_chunk=512),
        grid=(ni, nj),
        in_specs=[
            pl.BlockSpec((tm, D_MODEL), row, pipeline_mode=RESIDENT),
            pl.BlockSpec((tm, 1), row, pipeline_mode=RESIDENT),
            pl.BlockSpec((tm, MOBA_WIDTH), row, pipeline_mode=RESIDENT),
            pl.BlockSpec((tm, CONV_WIDTH), row, pipeline_mode=RESIDENT),
            pl.BlockSpec((tm, GLA_VW), row, pipeline_mode=RESIDENT),
            pl.BlockSpec((tn, D_MODEL), lambda i, j: (j, 0)),
            pl.BlockSpec((tn, D_MODEL), lambda i, j: (nj + j, 0)),
            pl.BlockSpec((tn, D_MODEL), lambda i, j: (2 * nj + j, 0)),
            pl.BlockSpec((1, tn), lambda i, j: (0, j)),
            pl.BlockSpec((1, tn), lambda i, j: (0, nj + j)),
            pl.BlockSpec((1, tn), lambda i, j: (0, 2 * nj + j)),
            pl.BlockSpec((None, MOBA_WIDTH, tn), colw),
            pl.BlockSpec((None, CONV_WIDTH, tn), colw),
            pl.BlockSpec((None, GLA_VW, tn), colw),
            pl.BlockSpec((None, slab, D_MODEL), lambda i, j: (layer, i * nj + j, 0)),
        ],
        out_specs=[
            pl.BlockSpec((tm, tn), lambda i, j: (i, j)),
            pl.BlockSpec((slab, D_MODEL), lambda i, j: (i * nj + j, 0)),
        ],
        out_shape=[
            jax.ShapeDtypeStruct((n_tok, D_MODEL), BF16),
            jax.ShapeDtypeStruct((D_MODEL, D_MODEL), BF16),
        ],
        compiler_params=_params(("arbitrary", "arbitrary")),
        name="merge_branches",
    )(hq, hdq, ya, yb, yc, w_gl, w_gl, w_gl, w_gl_dq, w_gl_dq, w_gl_dq, wa, wb, wc, w_out)


def _proj_residual_kernel(a_ref, w_ref, x_ref, gt_ref, o_ref):
    o_ref[...] = x_ref[...] + gt_ref[0] * _dot(a_ref[...], w_ref[...])


def proj_residual(a, w, x2, gt, tm, tn, name):
    m, k = a.shape
    n = w.shape[-1]
    per_batch = SEQ // tm
    return pl.pallas_call(
        _proj_residual_kernel,
        grid=(m // tm, n // tn),
        in_specs=[
            pl.BlockSpec((tm, k), lambda i, j: (i, 0), pipeline_mode=RESIDENT),
            pl.BlockSpec((k, tn), lambda i, j: (0, j)),
            pl.BlockSpec((tm, tn), lambda i, j: (i, j)),
            pl.BlockSpec((1, 1, tn), lambda i, j: (i // per_batch, 0, j)),
        ],
        out_specs=pl.BlockSpec((tm, tn), lambda i, j: (i, j)),
        out_shape=jax.ShapeDtypeStruct((m, n), F32),
        compiler_params=_params(("arbitrary", "arbitrary")),
        name=name,
    )(a, w, x2, gt)


def _ffn_up_kernel(a_ref, wg_ref, wu_ref, cw_ref, wd_ref, o_ref, wdb_ref, *, row_chunk):
    wdb_ref[...] = wd_ref[...].astype(wdb_ref.dtype)
    wg = wg_ref[...].astype(BF16)
    wu = wu_ref[...].astype(BF16)
    cw = cw_ref[...]
    prev = jnp.zeros((8, wg.shape[1]), F32)
    row = lax.broadcasted_iota(jnp.int32, (row_chunk, wg.shape[1]), 0)
    for r in range(a_ref.shape[0] // row_chunk):
        rows = slice(r * row_chunk, (r + 1) * row_chunk)
        a = a_ref[rows, :]
        g = _dot(a, wg)
        up = _dot(a, wu)
        s1 = jnp.where(row == 0, prev[7:8], pltpu.roll(g, 1, 0))
        s2 = jnp.where(row == 0, prev[6:7], jnp.where(row == 1, prev[7:8], pltpu.roll(g, 2, 0)))
        u = cw[2:3] * g + cw[1:2] * s1 + cw[0:1] * s2
        o_ref[rows, :] = (_silu(u) * up).astype(o_ref.dtype)
        prev = g[row_chunk - 8:row_chunk, :]


def ffn_up(h2, wg, wu, conv_w, w_down, layer):
    tm, tn = SEQ, 256
    n_tok = h2.shape[0]
    ni, nj = n_tok // tm, D_FF // tn
    slab = D_FF // (ni * nj)
    assert slab * ni * nj == D_FF and slab % 16 == 0
    return pl.pallas_call(
        functools.partial(_ffn_up_kernel, row_chunk=512),
        grid=(ni, nj),
        in_specs=[
            pl.BlockSpec((tm, D_MODEL), lambda i, j: (i, 0), pipeline_mode=RESIDENT),
            pl.BlockSpec((None, D_MODEL, tn), lambda i, j: (layer, 0, j)),
            pl.BlockSpec((None, D_MODEL, tn), lambda i, j: (layer, 0, j)),
            pl.BlockSpec((None, 3, tn), lambda i, j: (layer, 0, j)),
            pl.BlockSpec((None, slab, D_MODEL), lambda i, j: (layer, i * nj + j, 0)),
        ],
        out_specs=[
            pl.BlockSpec((tm, tn), lambda i, j: (i, j)),
            pl.BlockSpec((slab, D_MODEL), lambda i, j: (i * nj + j, 0)),
        ],
        out_shape=[
            jax.ShapeDtypeStruct((n_tok, D_FF), BF16),
            jax.ShapeDtypeStruct((D_FF, D_MODEL), BF16),
        ],
        compiler_params=_params(("arbitrary", "arbitrary")),
        name="ffn_up",
    )(h2, wg, wu, conv_w, w_down)


def kernel(x, c, norm1_g, w_ada, b_ada, w_in, conv_w, gla_a2, gla_a_bias, gla_norm_g,
           w_branch_a, w_branch_b, w_branch_c, w_out, norm2_g, w_ffn_gate, w_ffn_up,
           ffn_conv_w, w_ffn_down, final_norm_g):
    batch = x.shape[0]
    n_tok = batch * SEQ
    slopes = alibi_slopes(MOBA_HEADS)
    c_pad = jnp.zeros((8, D_MODEL), F32).at[:batch].set(c)
    mod = ada_modulation(c_pad, w_ada, b_ada)
    mod = mod[:, :batch].reshape(DEPTH, batch, 6, 1, D_MODEL)
    w_in_t = jnp.swapaxes(w_in, 1, 2)

    for l in range(DEPTH):
        sh1, sc1, gt1, sh2, sc2, gt2 = (mod[l, :, t] for t in range(6))

        a2_pad = jnp.zeros((LANES, GLA_KW), F32).at[:GLA_RANK].set(gla_a2[l])

        h, hq, hdq = norm_modulate(x, norm1_g[l], sc1, sh1, with_fp8=True)
        h2 = h.reshape(n_tok, D_MODEL)
        z, ac, w_gl, w_gl_dq = in_proj_main(h2, w_in_t, l)
        z3 = z.reshape(batch, SEQ, Z_WIDTH)

        ya = moba_attention(z3, slopes)
        yb = gated_conv(z3, conv_w[l])
        yc = gla_mixer(z3, ac.reshape(batch, SEQ, LANES), a2_pad, gla_a_bias[l], gla_norm_g[l])

        merged, w_out_b = merge_branches(
            hq.reshape(n_tok, D_MODEL), hdq.reshape(n_tok, 1),
            ya.reshape(n_tok, MOBA_WIDTH), yb.reshape(n_tok, CONV_WIDTH), yc.reshape(n_tok, GLA_VW),
            w_gl, w_gl_dq, w_branch_a, w_branch_b, w_branch_c, w_out, l)
        x2 = proj_residual(merged, w_out_b, x.reshape(n_tok, D_MODEL), gt1, 2048, 512, "out_proj")
        x = x2.reshape(batch, SEQ, D_MODEL)

        h = norm_modulate(x, norm2_g[l], sc2, sh2)
        act, w_down_b = ffn_up(h.reshape(n_tok, D_MODEL), w_ffn_gate, w_ffn_up, ffn_conv_w,
                               w_ffn_down, l)
        x2 = proj_residual(act, w_down_b, x2, gt2, 1024, 512, "ffn_down")
        x = x2.reshape(batch, SEQ, D_MODEL)

    return final_norm(x, final_norm_g)
```

```python
import functools
import math

import numpy as np
import jax
import jax.numpy as jnp
from jax import lax
from jax.experimental import pallas as pl
from jax.experimental.pallas import tpu as pltpu

F32 = jnp.float32
BF16 = jnp.bfloat16
F8 = jnp.float8_e4m3fn
F8_ROW_MAX = 256.0

D_MODEL = 4096
SEQ = 2048
DEPTH = 2

HEAD_DIM = 128
MOBA_HEADS = 12
MOBA_WIDTH = MOBA_HEADS * HEAD_DIM
MOBA_BLOCK = 256
MOBA_TOPK = 3
N_MOBA_BLOCKS = SEQ // MOBA_BLOCK
CONV_WIDTH = 1024
GLA_HEADS = 12
GLA_DK = 64
GLA_DV = 128
GLA_KW = GLA_HEADS * GLA_DK
GLA_VW = GLA_HEADS * GLA_DV
GLA_RANK = 16
GLA_TAU = 16.0
GLA_CHUNK = 64
D_FF = 11008
NORM_EPS = 1e-6

OFF_QA = 0
OFF_KA = OFF_QA + MOBA_WIDTH
OFF_VA = OFF_KA + MOBA_WIDTH
OFF_BX = OFF_VA + MOBA_WIDTH
OFF_BG = OFF_BX + CONV_WIDTH
OFF_CG = OFF_BG + CONV_WIDTH
OFF_QC = OFF_CG + CONV_WIDTH
OFF_KC = OFF_QC + GLA_KW
OFF_VC = OFF_KC + GLA_KW
OFF_RC = OFF_VC + GLA_VW
OFF_AC = OFF_RC + GLA_VW
OFF_GATE = OFF_AC + GLA_RANK
Z_WIDTH = OFF_AC

LANES = 128
VMEM_LIMIT = 56 * 1024 * 1024
RESIDENT = pl.Buffered(1)


def _params(sem):
    return pltpu.CompilerParams(dimension_semantics=sem, vmem_limit_bytes=VMEM_LIMIT)


def _dot(a, b):
    return jnp.dot(a, b, preferred_element_type=F32)


def _dot_nt(a, b):
    return lax.dot_general(a, b, (((1,), (1,)), ((), ())), preferred_element_type=F32)


def _dot_tn(a, b):
    return lax.dot_general(a, b, (((0,), (0,)), ((), ())), preferred_element_type=F32)


def _split_bf16(x, parts):
    out = []
    r = x
    for _ in range(parts):
        p = r.astype(BF16)
        out.append(p)
        r = r - p.astype(F32)
    return out


def _silu(x):
    return x * jax.nn.sigmoid(x)


def _quantize_rows(v):
    amax = jnp.max(jnp.abs(v), axis=-1, keepdims=True)
    q = (v * (F8_ROW_MAX / jnp.maximum(amax, 1e-30))).astype(F8)
    return q, amax * (1.0 / F8_ROW_MAX)


def alibi_slopes(n):
    def pow2(m):
        start = 2.0 ** (-(2.0 ** -(math.log2(m) - 3)))
        return [start ** (i + 1) for i in range(m)]
    if math.log2(n).is_integer():
        s = pow2(n)
    else:
        closest = 2 ** math.floor(math.log2(n))
        s = pow2(closest) + pow2(2 * closest)[0::2][: n - closest]
    return jnp.asarray(np.array(s, dtype=np.float32))


def _ada_tile(c_ref, w_ref, b_ref):
    return _dot(_silu(c_ref[...]).astype(BF16), w_ref[...].astype(BF16)) + b_ref[...]


def _ada_kernel(c_ref, w_ref, b_ref, o_ref):
    o_ref[...] = _ada_tile(c_ref, w_ref, b_ref)


def ada_modulation(c_pad, w_ada, b_ada3, layer):
    n_out = w_ada.shape[-1]
    tn = 512
    return pl.pallas_call(
        _ada_kernel,
        grid=(n_out // tn,),
        in_specs=[
            pl.BlockSpec((8, D_MODEL), lambda j: (0, 0)),
            pl.BlockSpec((None, D_MODEL, tn), lambda j: (layer, 0, j)),
            pl.BlockSpec((None, 1, tn), lambda j: (layer, 0, j)),
        ],
        out_specs=pl.BlockSpec((8, tn), lambda j: (0, j)),
        out_shape=jax.ShapeDtypeStruct((8, n_out), F32),
        compiler_params=_params(("arbitrary",)),
        name="ada_modulation",
    )(c_pad, w_ada, b_ada3)


def _norm_mod_kernel(x_ref, g_ref, sc_ref, sh_ref, o_ref):
    x = x_ref[0]
    y = x * lax.rsqrt(jnp.mean(x * x, axis=-1, keepdims=True) + NORM_EPS) * g_ref[...]
    o_ref[0] = (y * (1.0 + sc_ref[0]) + sh_ref[0]).astype(o_ref.dtype)


def _norm_mod_q_kernel(x_ref, g_ref, sc_ref, sh_ref, o_ref, q_ref, dq_ref):
    x = x_ref[0]
    y = x * lax.rsqrt(jnp.mean(x * x, axis=-1, keepdims=True) + NORM_EPS) * g_ref[...]
    h = y * (1.0 + sc_ref[0]) + sh_ref[0]
    o_ref[0] = h.astype(o_ref.dtype)
    q_ref[0], dq_ref[0] = _quantize_rows(h)


def _norm_kernel(x_ref, g_ref, o_ref):
    x = x_ref[0]
    y = x * lax.rsqrt(jnp.mean(x * x, axis=-1, keepdims=True) + NORM_EPS) * g_ref[...]
    o_ref[0] = y.astype(o_ref.dtype)


def norm_modulate(x, g, sc, sh, with_fp8=False):
    tm = 256
    batch = x.shape[0]
    tile = pl.BlockSpec((1, tm, D_MODEL), lambda b, i: (b, i, 0))
    out_specs, out_shape = tile, jax.ShapeDtypeStruct((batch, SEQ, D_MODEL), BF16)
    if with_fp8:
        out_specs = [tile, tile, pl.BlockSpec((1, tm, 1), lambda b, i: (b, i, 0))]
        out_shape = [out_shape, jax.ShapeDtypeStruct((batch, SEQ, D_MODEL), F8),
                     jax.ShapeDtypeStruct((batch, SEQ, 1), F32)]
    return pl.pallas_call(
        _norm_mod_q_kernel if with_fp8 else _norm_mod_kernel,
        grid=(batch, SEQ // tm),
        in_specs=[
            tile,
            pl.BlockSpec((1, D_MODEL), lambda b, i: (0, 0)),
            pl.BlockSpec((1, 1, D_MODEL), lambda b, i: (b, 0, 0)),
            pl.BlockSpec((1, 1, D_MODEL), lambda b, i: (b, 0, 0)),
        ],
        out_specs=out_specs,
        out_shape=out_shape,
        compiler_params=_params(("arbitrary", "arbitrary")),
        name="norm_modulate",
    )(x, g.reshape(1, D_MODEL), sc, sh)


def final_norm(x, g):
    tm = 256
    batch = x.shape[0]
    return pl.pallas_call(
        _norm_kernel,
        grid=(batch, SEQ // tm),
        in_specs=[
            pl.BlockSpec((1, tm, D_MODEL), lambda b, i: (b, i, 0)),
            pl.BlockSpec((1, D_MODEL), lambda b, i: (0, 0)),
        ],
        out_specs=pl.BlockSpec((1, tm, D_MODEL), lambda b, i: (b, i, 0)),
        out_shape=jax.ShapeDtypeStruct((batch, SEQ, D_MODEL), F32),
        compiler_params=_params(("arbitrary", "arbitrary")),
        name="final_norm",
    )(x, g.reshape(1, D_MODEL))


def _in_proj_main_kernel(a_ref, w_ref, wr_ref, gw_ref, o_ref, ac_ref, gwq_ref, gdq_ref):
    o_ref[...] = _dot_nt(a_ref[...], w_ref[...].astype(BF16)).astype(o_ref.dtype)

    @pl.when(pl.program_id(1) == 0)
    def _():
        ac_ref[...] = _dot_nt(a_ref[...], wr_ref[...].astype(BF16))

    q, dq = _quantize_rows(gw_ref[0])
    gwq_ref[...] = q
    n = dq.shape[0]
    r = lax.broadcasted_iota(jnp.int32, (n, n), 0)
    c = lax.broadcasted_iota(jnp.int32, (n, n), 1)
    gdq_ref[...] = jnp.sum(jnp.where(r == c, dq, 0.0), axis=0, keepdims=True)


def in_proj_main(a, w_in_t, layer):
    tm, tn = 2048, 512
    m, k = a.shape
    ni, nj = m // tm, Z_WIDTH // tn
    n_gate = 3 * D_MODEL
    slab = n_gate // (ni * nj)
    assert slab * ni * nj == n_gate and slab % LANES == 0
    return pl.pallas_call(
        _in_proj_main_kernel,
        grid=(ni, nj),
        in_specs=[
            pl.BlockSpec((tm, k), lambda i, j: (i, 0), pipeline_mode=RESIDENT),
            pl.BlockSpec((None, tn, k), lambda i, j: (layer, j, 0)),
            pl.BlockSpec((None, LANES, k), lambda i, j: (layer, OFF_AC // LANES, 0)),
            pl.BlockSpec((pl.Element(1), pl.Element(slab), pl.Element(k)),
                         lambda i, j: (layer, pl.multiple_of(OFF_GATE + (i * nj + j) * slab, GLA_RANK), 0)),
        ],
        out_specs=[
            pl.BlockSpec((tm, tn), lambda i, j: (i, j)),
            pl.BlockSpec((tm, LANES), lambda i, j: (i, 0)),
            pl.BlockSpec((slab, k), lambda i, j: (i * nj + j, 0)),
            pl.BlockSpec((1, slab), lambda i, j: (0, i * nj + j)),
        ],
        out_shape=[
            jax.ShapeDtypeStruct((m, Z_WIDTH), BF16),
            jax.ShapeDtypeStruct((m, LANES), F32),
            jax.ShapeDtypeStruct((n_gate, k), F8),
            jax.ShapeDtypeStruct((1, n_gate), F32),
        ],
        compiler_params=_params(("arbitrary", "arbitrary")),
        name="in_proj",
    )(a, w_in_t, w_in_t, w_in_t)


N_MOBA_PAIRS = N_MOBA_BLOCKS * (N_MOBA_BLOCKS + 1) // 2


def _moba_kernel(slopes_ref, q_ref, k_ref, v_ref, o_ref, s_ref, kt_ref, va_ref):
    slope = slopes_ref[pl.program_id(1)]
    scale = HEAD_DIM ** -0.5
    blk = MOBA_BLOCK
    nb = N_MOBA_BLOCKS

    r = lax.broadcasted_iota(jnp.int32, (LANES, SEQ), 0)
    t = lax.broadcasted_iota(jnp.int32, (LANES, SEQ), 1)
    pm = jnp.where(lax.shift_right_logical(t, blk.bit_length() - 1) == r, 1.0 / blk, 0.0).astype(BF16)
    kmh, kml = _split_bf16(_dot(pm, k_ref[0]), 2)
    kt_ref[...] = k_ref[0].astype(F32).T.astype(BF16)
    va_ref[:, :HEAD_DIM] = v_ref[0]
    va_ref[:, HEAD_DIM:] = jnp.ones((SEQ, HEAD_DIM), BF16)

    row = lax.broadcasted_iota(jnp.int32, (blk, blk), 0)
    col = lax.broadcasted_iota(jnp.int32, (blk, blk), 1)
    srel = slope * (col - row).astype(F32)
    causal = col <= row
    eye = jnp.where(row == col, 1.0, 0.0).astype(BF16)
    blk_id = lax.broadcasted_iota(jnp.int32, (nb, blk), 0)
    pad = jnp.zeros((LANES - nb, blk), F32)
    pair = {(qi, n): qi * (qi + 1) // 2 + n for qi in range(nb) for n in range(qi + 1)}

    m_runs = []
    for qi in range(nb):
        q_raw = q_ref[0, qi * blk:(qi + 1) * blk, :]
        qs = (q_raw.astype(F32) * scale).astype(BF16)

        sel_t = None
        if qi > MOBA_TOPK:
            g = (_dot_nt(kmh, q_raw) + _dot_nt(kml, q_raw))[:nb] * scale
            rank = jnp.zeros_like(g)
            for m in range(qi):
                gm = g[m:m + 1, :]
                tie = jnp.where(blk_id > m, 1.0, 0.0)
                rank = rank + jnp.where(gm > g, 1.0, jnp.where(gm == g, tie, 0.0))
            sel = jnp.where(rank < float(MOBA_TOPK), 1.0, 0.0)
            sel_t = _dot_nt(eye, jnp.concatenate([sel, pad], axis=0).astype(BF16))

        m_run = None
        for n in range(qi + 1):
            s = _dot(qs, kt_ref[:, n * blk:(n + 1) * blk]) + srel
            if n == qi:
                s = jnp.where(causal, s, -jnp.inf)
            elif sel_t is not None:
                s = jnp.where(sel_t[:, n:n + 1] > 0.5, s, -jnp.inf)
            s_ref[pair[(qi, n)]] = s
            rm = jnp.max(s, axis=1, keepdims=True) - slope * float(blk * (qi - n))
            m_run = rm if m_run is None else jnp.maximum(m_run, rm)
        m_runs.append(m_run)

    for qi in range(nb):
        acc = None
        for n in range(qi + 1):
            p = jnp.exp(s_ref[pair[(qi, n)]] - (m_runs[qi] + slope * float(blk * (qi - n))))
            pv = _dot(p.astype(BF16), va_ref[n * blk:(n + 1) * blk, :])
            acc = pv if acc is None else acc + pv
        o_ref[0, qi * blk:(qi + 1) * blk, :] = (acc[:, :HEAD_DIM] / acc[:, HEAD_DIM:]).astype(o_ref.dtype)


def moba_attention(z3, slopes):
    batch = z3.shape[0]
    cq, ck, cv = OFF_QA // HEAD_DIM, OFF_KA // HEAD_DIM, OFF_VA // HEAD_DIM
    return pl.pallas_call(
        _moba_kernel,
        grid=(batch, MOBA_HEADS),
        in_specs=[
            pl.BlockSpec(memory_space=pltpu.SMEM),
            pl.BlockSpec((1, SEQ, HEAD_DIM), lambda b, h: (b, 0, cq + h)),
            pl.BlockSpec((1, SEQ, HEAD_DIM), lambda b, h: (b, 0, ck + h)),
            pl.BlockSpec((1, SEQ, HEAD_DIM), lambda b, h: (b, 0, cv + h)),
        ],
        out_specs=pl.BlockSpec((1, SEQ, HEAD_DIM), lambda b, h: (b, 0, h)),
        out_shape=jax.ShapeDtypeStruct((batch, SEQ, MOBA_WIDTH), BF16),
        scratch_shapes=[
            pltpu.VMEM((N_MOBA_PAIRS, MOBA_BLOCK, MOBA_BLOCK), F32),
            pltpu.VMEM((HEAD_DIM, SEQ), BF16),
            pltpu.VMEM((SEQ, 2 * HEAD_DIM), BF16),
        ],
        compiler_params=_params(("arbitrary", "arbitrary")),
        name="moba_attention",
    )(slopes, z3, z3, z3)


def _shift_rows(x, k):
    rolled = pltpu.roll(x, k, 0)
    row = lax.broadcasted_iota(jnp.int32, x.shape, 0)
    return jnp.where(row < k, 0.0, rolled)


def _gconv_kernel(bx_ref, bg_ref, cg_ref, w_ref, o_ref):
    xin = cg_ref[0].astype(F32) * bx_ref[0].astype(F32)
    w = w_ref[...]
    y = w[2:3] * xin + w[1:2] * _shift_rows(xin, 1) + w[0:1] * _shift_rows(xin, 2)
    o_ref[0] = (bg_ref[0].astype(F32) * y).astype(o_ref.dtype)


def gated_conv(z3, conv_w):
    tc = 256
    batch = z3.shape[0]
    cx, cg_, cc = OFF_BX // tc, OFF_BG // tc, OFF_CG // tc
    return pl.pallas_call(
        _gconv_kernel,
        grid=(batch, CONV_WIDTH // tc),
        in_specs=[
            pl.BlockSpec((1, SEQ, tc), lambda b, j: (b, 0, cx + j)),
            pl.BlockSpec((1, SEQ, tc), lambda b, j: (b, 0, cg_ + j)),
            pl.BlockSpec((1, SEQ, tc), lambda b, j: (b, 0, cc + j)),
            pl.BlockSpec((3, tc), lambda b, j: (0, j)),
        ],
        out_specs=pl.BlockSpec((1, SEQ, tc), lambda b, j: (b, 0, j)),
        out_shape=jax.ShapeDtypeStruct((batch, SEQ, CONV_WIDTH), BF16),
        compiler_params=_params(("arbitrary", "arbitrary")),
        name="gated_conv",
    )(z3, z3, z3, conv_w)


GLA_GROUP = 256


def _gla_kernel(q_ref, k_ref, v_ref, r_ref, ac_ref, a2_ref, ab_ref, ng_ref, o_ref):
    cl, gr = GLA_CHUNK, GLA_GROUP
    n_grp, per = SEQ // gr, gr // cl

    lane = lax.broadcasted_iota(jnp.int32, (1, 2 * GLA_DK), 1)
    head0 = jnp.where(lane < GLA_DK, 1.0, 0.0)
    heads = (head0, 1.0 - head0)
    ri = lax.broadcasted_iota(jnp.int32, (gr, gr), 0)
    ci = lax.broadcasted_iota(jnp.int32, (gr, gr), 1)
    sh = cl.bit_length() - 1
    same_chunk = lax.shift_right_logical(ri, sh) == lax.shift_right_logical(ci, sh)
    causal = jnp.logical_and(same_chunk, ci <= ri)
    tril = jnp.where(causal, 1.0, 0.0).astype(BF16)
    sr = lax.broadcasted_iota(jnp.int32, (2 * GLA_DV, 2 * GLA_DK), 0)
    sc = lax.broadcasted_iota(jnp.int32, (2 * GLA_DV, 2 * GLA_DK), 1)
    same_head = jnp.where((sr < GLA_DV) == (sc < GLA_DK), 1.0, 0.0)

    a2b = a2_ref[...].astype(BF16)
    a_bias = ab_ref[...]
    ng = ng_ref[...]
    grp_rows = [slice(i * gr, (i + 1) * gr) for i in range(n_grp)]

    zz = [_dot(ac_ref[0, rw, :].astype(BF16), a2b) + a_bias for rw in grp_rows]
    gl = [(jnp.minimum(z, 0.0) - jnp.log1p(jnp.exp(-jnp.abs(z)))) * (1.0 / GLA_TAU) for z in zz]
    bcum = []
    for g in gl:
        g0, g1 = _split_bf16(g, 2)
        bcum.append(_dot(tril, g0) + _dot(tril, g1))
    q_i, k_i, k_dec, dec = [], [], [], []
    for rw, bc in zip(grp_rows, bcum):
        btot = jnp.concatenate(
            [jnp.broadcast_to(bc[(c + 1) * cl - 1:(c + 1) * cl, :], (cl, 2 * GLA_DK))
             for c in range(per)], axis=0)
        qf = q_ref[0, rw, :].astype(F32) * (GLA_DK ** -0.5)
        kf = k_ref[0, rw, :].astype(F32)
        q_i.append(qf * jnp.exp(bc))
        k_i.append((kf * jnp.exp(-bc)).astype(BF16))
        k_dec.append((kf * jnp.exp(btot - bc)).astype(BF16))
        dec.append(jnp.exp(btot))
    o_intra = []
    for rw, qg, kg in zip(grp_rows, q_i, k_i):
        halves = []
        for hh, hm in enumerate(heads):
            att = _dot_nt((qg * hm).astype(BF16), kg)
            att = jnp.where(causal, att, 0.0).astype(BF16)
            halves.append(_dot(att, v_ref[0, rw, hh * GLA_DV:(hh + 1) * GLA_DV]))
        o_intra.append(jnp.concatenate(halves, axis=1))
    kv = []
    for i, rw in enumerate(grp_rows):
        v = v_ref[0, rw, :]
        kv.append([_dot_tn(v[c * cl:(c + 1) * cl], k_dec[i][c * cl:(c + 1) * cl]) * same_head
                   for c in range(per)])

    st = jnp.zeros((2 * GLA_DV, 2 * GLA_DK), F32)
    for i, rw in enumerate(grp_rows):
        qb = q_i[i].astype(BF16)
        inter = []
        for c in range(per):
            inter.append(_dot_nt(qb[c * cl:(c + 1) * cl], st.astype(BF16)))
            st = st * dec[i][c * cl:c * cl + 1, :] + kv[i][c]
        o = o_intra[i] + jnp.concatenate(inter, axis=0)
        outs = []
        for hh in range(2):
            oh = o[:, hh * GLA_DV:(hh + 1) * GLA_DV]
            ms = jnp.mean(oh * oh, axis=-1, keepdims=True)
            outs.append(oh * lax.rsqrt(ms + NORM_EPS) * ng)
        y = jnp.concatenate(outs, axis=1)
        o_ref[0, rw, :] = (_silu(r_ref[0, rw, :].astype(F32)) * y).astype(o_ref.dtype)


def gla_mixer(z3, ac3, a2_pad, a_bias, norm_g):
    batch = z3.shape[0]
    kw, vw = 2 * GLA_DK, 2 * GLA_DV
    cq, ck, cv, cr = OFF_QC // kw, OFF_KC // kw, OFF_VC // vw, OFF_RC // vw
    return pl.pallas_call(
        _gla_kernel,
        grid=(batch, GLA_HEADS // 2),
        in_specs=[
            pl.BlockSpec((1, SEQ, kw), lambda b, j: (b, 0, cq + j)),
            pl.BlockSpec((1, SEQ, kw), lambda b, j: (b, 0, ck + j)),
            pl.BlockSpec((1, SEQ, vw), lambda b, j: (b, 0, cv + j)),
            pl.BlockSpec((1, SEQ, vw), lambda b, j: (b, 0, cr + j)),
            pl.BlockSpec((1, SEQ, LANES), lambda b, j: (b, 0, 0)),
            pl.BlockSpec((LANES, kw), lambda b, j: (0, j)),
            pl.BlockSpec((1, kw), lambda b, j: (0, j)),
            pl.BlockSpec((1, GLA_DV), lambda b, j: (0, 0)),
        ],
        out_specs=pl.BlockSpec((1, SEQ, vw), lambda b, j: (b, 0, j)),
        out_shape=jax.ShapeDtypeStruct((batch, SEQ, GLA_VW), BF16),
        compiler_params=_params(("arbitrary", "arbitrary")),
        name="gla_mixer",
    )(z3, z3, z3, z3, ac3, a2_pad, a_bias.reshape(1, GLA_KW), norm_g.reshape(1, GLA_DV))


def _merge_kernel(hq_ref, hdq_ref, ya_ref, yb_ref, yc_ref, wg0_ref, wg1_ref, wg2_ref,
                  dq0_ref, dq1_ref, dq2_ref, wa_ref, wb_ref, wc_ref, wo_ref, o_ref, wob_ref,
                  *, row_chunk):
    wob_ref[...] = wo_ref[...].astype(wob_ref.dtype)
    wa = wa_ref[...].astype(BF16)
    wb = wb_ref[...].astype(BF16)
    wc = wc_ref[...].astype(BF16)
    for r in range(hq_ref.shape[0] // row_chunk):
        rows = slice(r * row_chunk, (r + 1) * row_chunk)
        hq = hq_ref[rows, :]
        hdq = hdq_ref[rows, :]

        def gate(wg_ref, dq_ref):
            return jax.nn.sigmoid(_dot_nt(hq, wg_ref[...]) * hdq * dq_ref[...])

        acc = gate(wg0_ref, dq0_ref) * _dot(ya_ref[rows, :], wa)
        acc += gate(wg1_ref, dq1_ref) * _dot(yb_ref[rows, :], wb)
        acc += gate(wg2_ref, dq2_ref) * _dot(yc_ref[rows, :], wc)
        o_ref[rows, :] = acc.astype(o_ref.dtype)


def merge_branches(hq, hdq, ya, yb, yc, w_gl, w_gl_dq, wa, wb, wc, w_out, layer):
    tm, tn = 1024, 512
    n_tok = hq.shape[0]
    ni, nj = n_tok // tm, D_MODEL // tn
    slab = D_MODEL // (ni * nj)
    assert slab * ni * nj == D_MODEL and slab % 16 == 0
    row = lambda i, j: (i, 0)
    colw = lambda i, j: (layer, 0, j)
    return pl.pallas_call(
        functools.partial(_merge_kernel, row_chunk=512),
        grid=(ni, nj),
        in_specs=[
            pl.BlockSpec((tm, D_MODEL), row, pipeline_mode=RESIDENT),
            pl.BlockSpec((tm, 1), row, pipeline_mode=RESIDENT),
            pl.BlockSpec((tm, MOBA_WIDTH), row, pipeline_mode=RESIDENT),
            pl.BlockSpec((tm, CONV_WIDTH), row, pipeline_mode=RESIDENT),
            pl.BlockSpec((tm, GLA_VW), row, pipeline_mode=RESIDENT),
            pl.BlockSpec((tn, D_MODEL), lambda i, j: (j, 0)),
            pl.BlockSpec((tn, D_MODEL), lambda i, j: (nj + j, 0)),
            pl.BlockSpec((tn, D_MODEL), lambda i, j: (2 * nj + j, 0)),
            pl.BlockSpec((1, tn), lambda i, j: (0, j)),
            pl.BlockSpec((1, tn), lambda i, j: (0, nj + j)),
            pl.BlockSpec((1, tn), lambda i, j: (0, 2 * nj + j)),
            pl.BlockSpec((None, MOBA_WIDTH, tn), colw),
            pl.BlockSpec((None, CONV_WIDTH, tn), colw),
            pl.BlockSpec((None, GLA_VW, tn), colw),
            pl.BlockSpec((None, slab, D_MODEL), lambda i, j: (layer, i * nj + j, 0)),
        ],
        out_specs=[
            pl.BlockSpec((tm, tn), lambda i, j: (i, j)),
            pl.BlockSpec((slab, D_MODEL), lambda i, j: (i * nj + j, 0)),
        ],
        out_shape=[
            jax.ShapeDtypeStruct((n_tok, D_MODEL), BF16),
            jax.ShapeDtypeStruct((D_MODEL, D_MODEL), BF16),
        ],
        compiler_params=_params(("arbitrary", "arbitrary")),
        name="merge_branches",
    )(hq, hdq, ya, yb, yc, w_gl, w_gl, w_gl, w_gl_dq, w_gl_dq, w_gl_dq, wa, wb, wc, w_out)


def _proj_residual_kernel(a_ref, w_ref, x_ref, gt_ref, o_ref):
    o_ref[...] = x_ref[...] + gt_ref[0] * _dot(a_ref[...], w_ref[...])


def proj_residual(a, w, x2, gt, tm, tn, name):
    m, k = a.shape
    n = w.shape[-1]
    per_batch = SEQ // tm
    return pl.pallas_call(
        _proj_residual_kernel,
        grid=(m // tm, n // tn),
        in_specs=[
            pl.BlockSpec((tm, k), lambda i, j: (i, 0), pipeline_mode=RESIDENT),
            pl.BlockSpec((k, tn), lambda i, j: (0, j)),
            pl.BlockSpec((tm, tn), lambda i, j: (i, j)),
            pl.BlockSpec((1, 1, tn), lambda i, j: (i // per_batch, 0, j)),
        ],
        out_specs=pl.BlockSpec((tm, tn), lambda i, j: (i, j)),
        out_shape=jax.ShapeDtypeStruct((m, n), F32),
        compiler_params=_params(("arbitrary", "arbitrary")),
        name=name,
    )(a, w, x2, gt)


ADA_TILE = 256


def _ffn_up_kernel(a_ref, wg_ref, wu_ref, cw_ref, wd_ref, *rest, row_chunk, ada_steps):
    if ada_steps:
        c_ref, wada_ref, bada_ref, o_ref, wdb_ref, mod_ref = rest
        step = pl.program_id(0) * pl.num_programs(1) + pl.program_id(1)

        @pl.when(step < ada_steps)
        def _():
            mod_ref[...] = _ada_tile(c_ref, wada_ref, bada_ref)
    else:
        o_ref, wdb_ref = rest
    wdb_ref[...] = wd_ref[...].astype(wdb_ref.dtype)
    wg = wg_ref[...].astype(BF16)
    wu = wu_ref[...].astype(BF16)
    cw = cw_ref[...]
    prev = jnp.zeros((8, wg.shape[1]), F32)
    row = lax.broadcasted_iota(jnp.int32, (row_chunk, wg.shape[1]), 0)
    for r in range(a_ref.shape[0] // row_chunk):
        rows = slice(r * row_chunk, (r + 1) * row_chunk)
        a = a_ref[rows, :]
        g = _dot(a, wg)
        up = _dot(a, wu)
        s1 = jnp.where(row == 0, prev[7:8], pltpu.roll(g, 1, 0))
        s2 = jnp.where(row == 0, prev[6:7], jnp.where(row == 1, prev[7:8], pltpu.roll(g, 2, 0)))
        u = cw[2:3] * g + cw[1:2] * s1 + cw[0:1] * s2
        o_ref[rows, :] = (_silu(u) * up).astype(o_ref.dtype)
        prev = g[row_chunk - 8:row_chunk, :]


def ffn_up(h2, wg, wu, conv_w, w_down, layer, ada=None):
    tm, tn = SEQ, 256
    n_tok = h2.shape[0]
    ni, nj = n_tok // tm, D_FF // tn
    slab = D_FF // (ni * nj)
    assert slab * ni * nj == D_FF and slab % 16 == 0
    in_specs = [
        pl.BlockSpec((tm, D_MODEL), lambda i, j: (i, 0), pipeline_mode=RESIDENT),
        pl.BlockSpec((None, D_MODEL, tn), lambda i, j: (layer, 0, j)),
        pl.BlockSpec((None, D_MODEL, tn), lambda i, j: (layer, 0, j)),
        pl.BlockSpec((None, 3, tn), lambda i, j: (layer, 0, j)),
        pl.BlockSpec((None, slab, D_MODEL), lambda i, j: (layer, i * nj + j, 0)),
    ]
    out_specs = [
        pl.BlockSpec((tm, tn), lambda i, j: (i, j)),
        pl.BlockSpec((slab, D_MODEL), lambda i, j: (i * nj + j, 0)),
    ]
    out_shape = [
        jax.ShapeDtypeStruct((n_tok, D_FF), BF16),
        jax.ShapeDtypeStruct((D_FF, D_MODEL), BF16),
    ]
    operands = [h2, wg, wu, conv_w, w_down]
    ada_steps = 0
    if ada is not None:
        n_mod = ada[1].shape[-1]
        ada_steps = n_mod // ADA_TILE
        assert ada_steps * ADA_TILE == n_mod and ada_steps <= ni * nj
        tile = lambda i, j: jnp.minimum(i * nj + j, ada_steps - 1)
        in_specs += [
            pl.BlockSpec((8, D_MODEL), lambda i, j: (0, 0)),
            pl.BlockSpec((None, D_MODEL, ADA_TILE), lambda i, j: (layer + 1, 0, tile(i, j))),
            pl.BlockSpec((None, 1, ADA_TILE), lambda i, j: (layer + 1, 0, tile(i, j))),
        ]
        out_specs.append(pl.BlockSpec((8, ADA_TILE), lambda i, j: (0, tile(i, j))))
        out_shape.append(jax.ShapeDtypeStruct((8, n_mod), F32))
        operands += list(ada)
    return pl.pallas_call(
        functools.partial(_ffn_up_kernel, row_chunk=512, ada_steps=ada_steps),
        grid=(ni, nj),
        in_specs=in_specs,
        out_specs=out_specs,
        out_shape=out_shape,
        compiler_params=_params(("arbitrary", "arbitrary")),
        name="ffn_up",
    )(*operands)


def kernel(x, c, norm1_g, w_ada, b_ada, w_in, conv_w, gla_a2, gla_a_bias, gla_norm_g,
           w_branch_a, w_branch_b, w_branch_c, w_out, norm2_g, w_ffn_gate, w_ffn_up,
           ffn_conv_w, w_ffn_down, final_norm_g):
    batch = x.shape[0]
    n_tok = batch * SEQ
    slopes = alibi_slopes(MOBA_HEADS)
    c_pad = jnp.zeros((8, D_MODEL), F32).at[:batch].set(c)
    b_ada3 = b_ada.reshape(DEPTH, 1, b_ada.shape[-1])
    mod = ada_modulation(c_pad, w_ada, b_ada3, 0)
    w_in_t = jnp.swapaxes(w_in, 1, 2)

    for l in range(DEPTH):
        mod = mod[:batch].reshape(batch, 6, 1, D_MODEL)
        sh1, sc1, gt1, sh2, sc2, gt2 = (mod[:, t] for t in range(6))

        a2_pad = jnp.zeros((LANES, GLA_KW), F32).at[:GLA_RANK].set(gla_a2[l])

        h, hq, hdq = norm_modulate(x, norm1_g[l], sc1, sh1, with_fp8=True)
        h2 = h.reshape(n_tok, D_MODEL)
        z, ac, w_gl, w_gl_dq = in_proj_main(h2, w_in_t, l)
        z3 = z.reshape(batch, SEQ, Z_WIDTH)

        ya = moba_attention(z3, slopes)
        yb = gated_conv(z3, conv_w[l])
        yc = gla_mixer(z3, ac.reshape(batch, SEQ, LANES), a2_pad, gla_a_bias[l], gla_norm_g[l])

        merged, w_out_b = merge_branches(
            hq.reshape(n_tok, D_MODEL), hdq.reshape(n_tok, 1),
            ya.reshape(n_tok, MOBA_WIDTH), yb.reshape(n_tok, CONV_WIDTH), yc.reshape(n_tok, GLA_VW),
            w_gl, w_gl_dq, w_branch_a, w_branch_b, w_branch_c, w_out, l)
        x2 = proj_residual(merged, w_out_b, x.reshape(n_tok, D_MODEL), gt1, 2048, 512, "out_proj")
        x = x2.reshape(batch, SEQ, D_MODEL)

        h = norm_modulate(x, norm2_g[l], sc2, sh2)
        ada = (c_pad, w_ada, b_ada3) if l + 1 < DEPTH else None
        act, w_down_b, *mod_next = ffn_up(h.reshape(n_tok, D_MODEL), w_ffn_gate, w_ffn_up,
                                          ffn_conv_w, w_ffn_down, l, ada)
        if mod_next:
            mod = mod_next[0]
        x2 = proj_residual(act, w_down_b, x2, gt2, 1024, 512, "ffn_down")
        x = x2.reshape(batch, SEQ, D_MODEL)

    return final_norm(x, final_norm_g)
```

```python
import functools
import math

import numpy as np
import jax
import jax.numpy as jnp
from jax import lax
from jax.experimental import pallas as pl
from jax.experimental.pallas import tpu as pltpu

F32 = jnp.float32
BF16 = jnp.bfloat16
F8 = jnp.float8_e4m3fn
F8_ROW_MAX = 256.0

D_MODEL = 4096
SEQ = 2048
DEPTH = 2

HEAD_DIM = 128
MOBA_HEADS = 12
MOBA_WIDTH = MOBA_HEADS * HEAD_DIM
MOBA_BLOCK = 256
MOBA_TOPK = 3
N_MOBA_BLOCKS = SEQ // MOBA_BLOCK
CONV_WIDTH = 1024
GLA_HEADS = 12
GLA_DK = 64
GLA_DV = 128
GLA_KW = GLA_HEADS * GLA_DK
GLA_VW = GLA_HEADS * GLA_DV
GLA_RANK = 16
GLA_TAU = 16.0
GLA_CHUNK = 64
D_FF = 11008
NORM_EPS = 1e-6

OFF_QA = 0
OFF_KA = OFF_QA + MOBA_WIDTH
OFF_VA = OFF_KA + MOBA_WIDTH
OFF_BX = OFF_VA + MOBA_WIDTH
OFF_BG = OFF_BX + CONV_WIDTH
OFF_CG = OFF_BG + CONV_WIDTH
OFF_QC = OFF_CG + CONV_WIDTH
OFF_KC = OFF_QC + GLA_KW
OFF_VC = OFF_KC + GLA_KW
OFF_RC = OFF_VC + GLA_VW
OFF_AC = OFF_RC + GLA_VW
OFF_GATE = OFF_AC + GLA_RANK
Z_WIDTH = OFF_AC

LANES = 128
VMEM_LIMIT = 56 * 1024 * 1024
RESIDENT = pl.Buffered(1)


def _params(sem):
    return pltpu.CompilerParams(dimension_semantics=sem, vmem_limit_bytes=VMEM_LIMIT)


def _dot(a, b):
    return jnp.dot(a, b, preferred_element_type=F32)


def _dot_nt(a, b):
    return lax.dot_general(a, b, (((1,), (1,)), ((), ())), preferred_element_type=F32)


def _dot_tn(a, b):
    return lax.dot_general(a, b, (((0,), (0,)), ((), ())), preferred_element_type=F32)


def _split_bf16(x, parts):
    out = []
    r = x
    for _ in range(parts):
        p = r.astype(BF16)
        out.append(p)
        r = r - p.astype(F32)
    return out


def _sigmoid(x):
    return 0.5 * jnp.tanh(0.5 * x) + 0.5


def _silu(x):
    t = 0.5 * x
    return t * jnp.tanh(t) + t


def _quantize_rows(v):
    amax = jnp.max(jnp.abs(v), axis=-1, keepdims=True)
    q = (v * (F8_ROW_MAX / jnp.maximum(amax, 1e-30))).astype(F8)
    return q, amax * (1.0 / F8_ROW_MAX)


def alibi_slopes(n):
    def pow2(m):
        start = 2.0 ** (-(2.0 ** -(math.log2(m) - 3)))
        return [start ** (i + 1) for i in range(m)]
    if math.log2(n).is_integer():
        s = pow2(n)
    else:
        closest = 2 ** math.floor(math.log2(n))
        s = pow2(closest) + pow2(2 * closest)[0::2][: n - closest]
    return jnp.asarray(np.array(s, dtype=np.float32))


def _ada_tile(c_ref, w_ref, b_ref):
    return _dot(_silu(c_ref[...]).astype(BF16), w_ref[...].astype(BF16)) + b_ref[...]


def _ada_kernel(c_ref, w_ref, b_ref, o_ref):
    o_ref[...] = _ada_tile(c_ref, w_ref, b_ref)


def ada_modulation(c_pad, w_ada, b_ada3, layer):
    n_out = w_ada.shape[-1]
    tn = 512
    return pl.pallas_call(
        _ada_kernel,
        grid=(n_out // tn,),
        in_specs=[
            pl.BlockSpec((8, D_MODEL), lambda j: (0, 0)),
            pl.BlockSpec((None, D_MODEL, tn), lambda j: (layer, 0, j)),
            pl.BlockSpec((None, 1, tn), lambda j: (layer, 0, j)),
        ],
        out_specs=pl.BlockSpec((8, tn), lambda j: (0, j)),
        out_shape=jax.ShapeDtypeStruct((8, n_out), F32),
        compiler_params=_params(("arbitrary",)),
        name="ada_modulation",
    )(c_pad, w_ada, b_ada3)


def _norm_mod_kernel(x_ref, g_ref, sc_ref, sh_ref, o_ref):
    x = x_ref[0]
    y = x * lax.rsqrt(jnp.mean(x * x, axis=-1, keepdims=True) + NORM_EPS) * g_ref[...]
    o_ref[0] = (y * (1.0 + sc_ref[0]) + sh_ref[0]).astype(o_ref.dtype)


def _norm_mod_q_kernel(x_ref, g_ref, sc_ref, sh_ref, o_ref, q_ref, dq_ref):
    x = x_ref[0]
    y = x * lax.rsqrt(jnp.mean(x * x, axis=-1, keepdims=True) + NORM_EPS) * g_ref[...]
    h = y * (1.0 + sc_ref[0]) + sh_ref[0]
    o_ref[0] = h.astype(o_ref.dtype)
    q_ref[0], dq_ref[0] = _quantize_rows(h)


def _norm_kernel(x_ref, g_ref, o_ref):
    x = x_ref[0]
    y = x * lax.rsqrt(jnp.mean(x * x, axis=-1, keepdims=True) + NORM_EPS) * g_ref[...]
    o_ref[0] = y.astype(o_ref.dtype)


def norm_modulate(x, g, sc, sh, with_fp8=False):
    tm = 256
    batch = x.shape[0]
    tile = pl.BlockSpec((1, tm, D_MODEL), lambda b, i: (b, i, 0))
    out_specs, out_shape = tile, jax.ShapeDtypeStruct((batch, SEQ, D_MODEL), BF16)
    if with_fp8:
        out_specs = [tile, tile, pl.BlockSpec((1, tm, 1), lambda b, i: (b, i, 0))]
        out_shape = [out_shape, jax.ShapeDtypeStruct((batch, SEQ, D_MODEL), F8),
                     jax.ShapeDtypeStruct((batch, SEQ, 1), F32)]
    return pl.pallas_call(
        _norm_mod_q_kernel if with_fp8 else _norm_mod_kernel,
        grid=(batch, SEQ // tm),
        in_specs=[
            tile,
            pl.BlockSpec((1, D_MODEL), lambda b, i: (0, 0)),
            pl.BlockSpec((1, 1, D_MODEL), lambda b, i: (b, 0, 0)),
            pl.BlockSpec((1, 1, D_MODEL), lambda b, i: (b, 0, 0)),
        ],
        out_specs=out_specs,
        out_shape=out_shape,
        compiler_params=_params(("arbitrary", "arbitrary")),
        name="norm_modulate",
    )(x, g.reshape(1, D_MODEL), sc, sh)


def final_norm(x, g):
    tm = 256
    batch = x.shape[0]
    return pl.pallas_call(
        _norm_kernel,
        grid=(batch, SEQ // tm),
        in_specs=[
            pl.BlockSpec((1, tm, D_MODEL), lambda b, i: (b, i, 0)),
            pl.BlockSpec((1, D_MODEL), lambda b, i: (0, 0)),
        ],
        out_specs=pl.BlockSpec((1, tm, D_MODEL), lambda b, i: (b, i, 0)),
        out_shape=jax.ShapeDtypeStruct((batch, SEQ, D_MODEL), F32),
        compiler_params=_params(("arbitrary", "arbitrary")),
        name="final_norm",
    )(x, g.reshape(1, D_MODEL))


def _in_proj_main_kernel(a_ref, w_ref, wr_ref, gw_ref, o_ref, ac_ref, gwq_ref, gdq_ref):
    o_ref[...] = _dot_nt(a_ref[...], w_ref[...].astype(BF16)).astype(o_ref.dtype)

    @pl.when(pl.program_id(1) == 0)
    def _():
        ac_ref[...] = _dot_nt(a_ref[...], wr_ref[...].astype(BF16))

    q, dq = _quantize_rows(gw_ref[0])
    gwq_ref[...] = q
    n = dq.shape[0]
    r = lax.broadcasted_iota(jnp.int32, (n, n), 0)
    c = lax.broadcasted_iota(jnp.int32, (n, n), 1)
    gdq_ref[...] = jnp.sum(jnp.where(r == c, dq, 0.0), axis=0, keepdims=True)


def in_proj_main(a, w_in_t, layer):
    tm, tn = 2048, 512
    m, k = a.shape
    ni, nj = m // tm, Z_WIDTH // tn
    n_gate = 3 * D_MODEL
    slab = n_gate // (ni * nj)
    assert slab * ni * nj == n_gate and slab % LANES == 0
    return pl.pallas_call(
        _in_proj_main_kernel,
        grid=(ni, nj),
        in_specs=[
            pl.BlockSpec((tm, k), lambda i, j: (i, 0), pipeline_mode=RESIDENT),
            pl.BlockSpec((None, tn, k), lambda i, j: (layer, j, 0)),
            pl.BlockSpec((None, LANES, k), lambda i, j: (layer, OFF_AC // LANES, 0)),
            pl.BlockSpec((pl.Element(1), pl.Element(slab), pl.Element(k)),
                         lambda i, j: (layer, pl.multiple_of(OFF_GATE + (i * nj + j) * slab, GLA_RANK), 0)),
        ],
        out_specs=[
            pl.BlockSpec((tm, tn), lambda i, j: (i, j)),
            pl.BlockSpec((tm, LANES), lambda i, j: (i, 0)),
            pl.BlockSpec((slab, k), lambda i, j: (i * nj + j, 0)),
            pl.BlockSpec((1, slab), lambda i, j: (0, i * nj + j)),
        ],
        out_shape=[
            jax.ShapeDtypeStruct((m, Z_WIDTH), BF16),
            jax.ShapeDtypeStruct((m, LANES), F32),
            jax.ShapeDtypeStruct((n_gate, k), F8),
            jax.ShapeDtypeStruct((1, n_gate), F32),
        ],
        compiler_params=_params(("arbitrary", "arbitrary")),
        name="in_proj",
    )(a, w_in_t, w_in_t, w_in_t)


N_MOBA_PAIRS = N_MOBA_BLOCKS * (N_MOBA_BLOCKS + 1) // 2


def _moba_kernel(slopes_ref, q_ref, k_ref, v_ref, o_ref, s_ref, kt_ref, va_ref):
    slope = slopes_ref[pl.program_id(1)]
    scale = HEAD_DIM ** -0.5
    blk = MOBA_BLOCK
    nb = N_MOBA_BLOCKS

    r = lax.broadcasted_iota(jnp.int32, (LANES, SEQ), 0)
    t = lax.broadcasted_iota(jnp.int32, (LANES, SEQ), 1)
    pm = jnp.where(lax.shift_right_logical(t, blk.bit_length() - 1) == r, 1.0 / blk, 0.0).astype(BF16)
    kmh, kml = _split_bf16(_dot(pm, k_ref[0]), 2)
    kt_ref[...] = k_ref[0].astype(F32).T.astype(BF16)
    va_ref[:, :HEAD_DIM] = v_ref[0]
    va_ref[:, HEAD_DIM:] = jnp.ones((SEQ, HEAD_DIM), BF16)

    row = lax.broadcasted_iota(jnp.int32, (blk, blk), 0)
    col = lax.broadcasted_iota(jnp.int32, (blk, blk), 1)
    srel = slope * (col - row).astype(F32)
    causal = col <= row
    eye = jnp.where(row == col, 1.0, 0.0).astype(BF16)
    blk_id = lax.broadcasted_iota(jnp.int32, (nb, blk), 0)
    pad = jnp.zeros((LANES - nb, blk), F32)
    pair = {(qi, n): qi * (qi + 1) // 2 + n for qi in range(nb) for n in range(qi + 1)}

    m_runs = []
    for qi in range(nb):
        q_raw = q_ref[0, qi * blk:(qi + 1) * blk, :]
        qs = (q_raw.astype(F32) * scale).astype(BF16)

        sel_t = None
        if qi > MOBA_TOPK:
            g = (_dot_nt(kmh, q_raw) + _dot_nt(kml, q_raw))[:nb] * scale
            rank = jnp.zeros_like(g)
            for m in range(qi):
                gm = g[m:m + 1, :]
                tie = jnp.where(blk_id > m, 1.0, 0.0)
                rank = rank + jnp.where(gm > g, 1.0, jnp.where(gm == g, tie, 0.0))
            sel = jnp.where(rank < float(MOBA_TOPK), 1.0, 0.0)
            sel_t = _dot_nt(eye, jnp.concatenate([sel, pad], axis=0).astype(BF16))

        m_run = None
        for n in range(qi + 1):
            s = _dot(qs, kt_ref[:, n * blk:(n + 1) * blk]) + srel
            if n == qi:
                s = jnp.where(causal, s, -jnp.inf)
            elif sel_t is not None:
                s = jnp.where(sel_t[:, n:n + 1] > 0.5, s, -jnp.inf)
            s_ref[pair[(qi, n)]] = s
            rm = jnp.max(s, axis=1, keepdims=True) - slope * float(blk * (qi - n))
            m_run = rm if m_run is None else jnp.maximum(m_run, rm)
        m_runs.append(m_run)

    for qi in range(nb):
        acc = None
        for n in range(qi + 1):
            p = jnp.exp(s_ref[pair[(qi, n)]] - (m_runs[qi] + slope * float(blk * (qi - n))))
            pv = _dot(p.astype(BF16), va_ref[n * blk:(n + 1) * blk, :])
            acc = pv if acc is None else acc + pv
        o_ref[0, qi * blk:(qi + 1) * blk, :] = (acc[:, :HEAD_DIM] / acc[:, HEAD_DIM:]).astype(o_ref.dtype)


def moba_attention(z3, slopes):
    batch = z3.shape[0]
    cq, ck, cv = OFF_QA // HEAD_DIM, OFF_KA // HEAD_DIM, OFF_VA // HEAD_DIM
    return pl.pallas_call(
        _moba_kernel,
        grid=(batch, MOBA_HEADS),
        in_specs=[
            pl.BlockSpec(memory_space=pltpu.SMEM),
            pl.BlockSpec((1, SEQ, HEAD_DIM), lambda b, h: (b, 0, cq + h)),
            pl.BlockSpec((1, SEQ, HEAD_DIM), lambda b, h: (b, 0, ck + h)),
            pl.BlockSpec((1, SEQ, HEAD_DIM), lambda b, h: (b, 0, cv + h)),
        ],
        out_specs=pl.BlockSpec((1, SEQ, HEAD_DIM), lambda b, h: (b, 0, h)),
        out_shape=jax.ShapeDtypeStruct((batch, SEQ, MOBA_WIDTH), BF16),
        scratch_shapes=[
            pltpu.VMEM((N_MOBA_PAIRS, MOBA_BLOCK, MOBA_BLOCK), F32),
            pltpu.VMEM((HEAD_DIM, SEQ), BF16),
            pltpu.VMEM((SEQ, 2 * HEAD_DIM), BF16),
        ],
        compiler_params=_params(("arbitrary", "arbitrary")),
        name="moba_attention",
    )(slopes, z3, z3, z3)


def _shift_rows(x, k):
    rolled = pltpu.roll(x, k, 0)
    row = lax.broadcasted_iota(jnp.int32, x.shape, 0)
    return jnp.where(row < k, 0.0, rolled)


def _gconv_kernel(bx_ref, bg_ref, cg_ref, w_ref, o_ref):
    xin = cg_ref[0].astype(F32) * bx_ref[0].astype(F32)
    w = w_ref[...]
    y = w[2:3] * xin + w[1:2] * _shift_rows(xin, 1) + w[0:1] * _shift_rows(xin, 2)
    o_ref[0] = (bg_ref[0].astype(F32) * y).astype(o_ref.dtype)


def gated_conv(z3, conv_w):
    tc = 256
    batch = z3.shape[0]
    cx, cg_, cc = OFF_BX // tc, OFF_BG // tc, OFF_CG // tc
    return pl.pallas_call(
        _gconv_kernel,
        grid=(batch, CONV_WIDTH // tc),
        in_specs=[
            pl.BlockSpec((1, SEQ, tc), lambda b, j: (b, 0, cx + j)),
            pl.BlockSpec((1, SEQ, tc), lambda b, j: (b, 0, cg_ + j)),
            pl.BlockSpec((1, SEQ, tc), lambda b, j: (b, 0, cc + j)),
            pl.BlockSpec((3, tc), lambda b, j: (0, j)),
        ],
        out_specs=pl.BlockSpec((1, SEQ, tc), lambda b, j: (b, 0, j)),
        out_shape=jax.ShapeDtypeStruct((batch, SEQ, CONV_WIDTH), BF16),
        compiler_params=_params(("arbitrary", "arbitrary")),
        name="gated_conv",
    )(z3, z3, z3, conv_w)


GLA_GROUP = 256


def _gla_kernel(q_ref, k_ref, v_ref, r_ref, ac_ref, a2_ref, ab_ref, ng_ref, o_ref):
    cl, gr = GLA_CHUNK, GLA_GROUP
    n_grp, per = SEQ // gr, gr // cl

    lane = lax.broadcasted_iota(jnp.int32, (1, 2 * GLA_DK), 1)
    head0 = jnp.where(lane < GLA_DK, 1.0, 0.0)
    heads = (head0, 1.0 - head0)
    ri = lax.broadcasted_iota(jnp.int32, (gr, gr), 0)
    ci = lax.broadcasted_iota(jnp.int32, (gr, gr), 1)
    sh = cl.bit_length() - 1
    same_chunk = lax.shift_right_logical(ri, sh) == lax.shift_right_logical(ci, sh)
    causal = jnp.logical_and(same_chunk, ci <= ri)
    tril = jnp.where(causal, 1.0, 0.0).astype(BF16)
    sr = lax.broadcasted_iota(jnp.int32, (2 * GLA_DV, 2 * GLA_DK), 0)
    sc = lax.broadcasted_iota(jnp.int32, (2 * GLA_DV, 2 * GLA_DK), 1)
    same_head = jnp.where((sr < GLA_DV) == (sc < GLA_DK), 1.0, 0.0)

    a2b = a2_ref[...].astype(BF16)
    a_bias = ab_ref[...]
    ng = ng_ref[...]
    grp_rows = [slice(i * gr, (i + 1) * gr) for i in range(n_grp)]

    zz = [_dot(ac_ref[0, rw, :].astype(BF16), a2b) + a_bias for rw in grp_rows]
    gl = [(jnp.minimum(z, 0.0) - jnp.log1p(jnp.exp(-jnp.abs(z)))) * (1.0 / GLA_TAU) for z in zz]
    bcum = []
    for g in gl:
        g0, g1 = _split_bf16(g, 2)
        bcum.append(_dot(tril, g0) + _dot(tril, g1))
    q_i, k_i, k_dec, dec = [], [], [], []
    for rw, bc in zip(grp_rows, bcum):
        btot = jnp.concatenate(
            [jnp.broadcast_to(bc[(c + 1) * cl - 1:(c + 1) * cl, :], (cl, 2 * GLA_DK))
             for c in range(per)], axis=0)
        qf = q_ref[0, rw, :].astype(F32) * (GLA_DK ** -0.5)
        kf = k_ref[0, rw, :].astype(F32)
        q_i.append(qf * jnp.exp(bc))
        k_i.append((kf * jnp.exp(-bc)).astype(BF16))
        k_dec.append((kf * jnp.exp(btot - bc)).astype(BF16))
        dec.append(jnp.exp(btot))
    o_intra = []
    for rw, qg, kg in zip(grp_rows, q_i, k_i):
        halves = []
        for hh, hm in enumerate(heads):
            att = _dot_nt((qg * hm).astype(BF16), kg)
            att = jnp.where(causal, att, 0.0).astype(BF16)
            halves.append(_dot(att, v_ref[0, rw, hh * GLA_DV:(hh + 1) * GLA_DV]))
        o_intra.append(jnp.concatenate(halves, axis=1))
    kv = []
    for i, rw in enumerate(grp_rows):
        v = v_ref[0, rw, :]
        kv.append([_dot_tn(v[c * cl:(c + 1) * cl], k_dec[i][c * cl:(c + 1) * cl]) * same_head
                   for c in range(per)])

    st = jnp.zeros((2 * GLA_DV, 2 * GLA_DK), F32)
    for i, rw in enumerate(grp_rows):
        qb = q_i[i].astype(BF16)
        inter = []
        for c in range(per):
            inter.append(_dot_nt(qb[c * cl:(c + 1) * cl], st.astype(BF16)))
            st = st * dec[i][c * cl:c * cl + 1, :] + kv[i][c]
        o = o_intra[i] + jnp.concatenate(inter, axis=0)
        outs = []
        for hh in range(2):
            oh = o[:, hh * GLA_DV:(hh + 1) * GLA_DV]
            ms = jnp.mean(oh * oh, axis=-1, keepdims=True)
            outs.append(oh * lax.rsqrt(ms + NORM_EPS) * ng)
        y = jnp.concatenate(outs, axis=1)
        o_ref[0, rw, :] = (_silu(r_ref[0, rw, :].astype(F32)) * y).astype(o_ref.dtype)


def gla_mixer(z3, ac3, a2_pad, a_bias, norm_g):
    batch = z3.shape[0]
    kw, vw = 2 * GLA_DK, 2 * GLA_DV
    cq, ck, cv, cr = OFF_QC // kw, OFF_KC // kw, OFF_VC // vw, OFF_RC // vw
    return pl.pallas_call(
        _gla_kernel,
        grid=(batch, GLA_HEADS // 2),
        in_specs=[
            pl.BlockSpec((1, SEQ, kw), lambda b, j: (b, 0, cq + j)),
            pl.BlockSpec((1, SEQ, kw), lambda b, j: (b, 0, ck + j)),
            pl.BlockSpec((1, SEQ, vw), lambda b, j: (b, 0, cv + j)),
            pl.BlockSpec((1, SEQ, vw), lambda b, j: (b, 0, cr + j)),
            pl.BlockSpec((1, SEQ, LANES), lambda b, j: (b, 0, 0)),
            pl.BlockSpec((LANES, kw), lambda b, j: (0, j)),
            pl.BlockSpec((1, kw), lambda b, j: (0, j)),
            pl.BlockSpec((1, GLA_DV), lambda b, j: (0, 0)),
        ],
        out_specs=pl.BlockSpec((1, SEQ, vw), lambda b, j: (b, 0, j)),
        out_shape=jax.ShapeDtypeStruct((batch, SEQ, GLA_VW), BF16),
        compiler_params=_params(("arbitrary", "arbitrary")),
        name="gla_mixer",
    )(z3, z3, z3, z3, ac3, a2_pad, a_bias.reshape(1, GLA_KW), norm_g.reshape(1, GLA_DV))


def _merge_kernel(hq_ref, hdq_ref, ya_ref, yb_ref, yc_ref, wg0_ref, wg1_ref, wg2_ref,
                  dq0_ref, dq1_ref, dq2_ref, wa_ref, wb_ref, wc_ref, wo_ref, o_ref, wob_ref,
                  *, row_chunk):
    wob_ref[...] = wo_ref[...].astype(wob_ref.dtype)
    wa = wa_ref[...].astype(BF16)
    wb = wb_ref[...].astype(BF16)
    wc = wc_ref[...].astype(BF16)
    for r in range(hq_ref.shape[0] // row_chunk):
        rows = slice(r * row_chunk, (r + 1) * row_chunk)
        hq = hq_ref[rows, :]
        hdq = hdq_ref[rows, :]

        def gate(wg_ref, dq_ref):
            return _sigmoid(_dot_nt(hq, wg_ref[...]) * hdq * dq_ref[...])

        acc = gate(wg0_ref, dq0_ref) * _dot(ya_ref[rows, :], wa)
        acc += gate(wg1_ref, dq1_ref) * _dot(yb_ref[rows, :], wb)
        acc += gate(wg2_ref, dq2_ref) * _dot(yc_ref[rows, :], wc)
        o_ref[rows, :] = acc.astype(o_ref.dtype)


def merge_branches(hq, hdq, ya, yb, yc, w_gl, w_gl_dq, wa, wb, wc, w_out, layer):
    tm, tn = 1024, 512
    n_tok = hq.shape[0]
    ni, nj = n_tok // tm, D_MODEL // tn
    slab = D_MODEL // (ni * nj)
    assert slab * ni * nj == D_MODEL and slab % 16 == 0
    row = lambda i, j: (i, 0)
    colw = lambda i, j: (layer, 0, j)
    return pl.pallas_call(
        functools.partial(_merge_kernel, row_chunk=512),
        grid=(ni, nj),
        in_specs=[
            pl.BlockSpec((tm, D_MODEL), row, pipeline_mode=RESIDENT),
            pl.BlockSpec((tm, 1), row, pipeline_mode=RESIDENT),
            pl.BlockSpec((tm, MOBA_WIDTH), row, pipeline_mode=RESIDENT),
            pl.BlockSpec((tm, CONV_WIDTH), row, pipeline_mode=RESIDENT),
            pl.BlockSpec((tm, GLA_VW), row, pipeline_mode=RESIDENT),
            pl.BlockSpec((tn, D_MODEL), lambda i, j: (j, 0)),
            pl.BlockSpec((tn, D_MODEL), lambda i, j: (nj + j, 0)),
            pl.BlockSpec((tn, D_MODEL), lambda i, j: (2 * nj + j, 0)),
            pl.BlockSpec((1, tn), lambda i, j: (0, j)),
            pl.BlockSpec((1, tn), lambda i, j: (0, nj + j)),
            pl.BlockSpec((1, tn), lambda i, j: (0, 2 * nj + j)),
            pl.BlockSpec((None, MOBA_WIDTH, tn), colw),
            pl.BlockSpec((None, CONV_WIDTH, tn), colw),
            pl.BlockSpec((None, GLA_VW, tn), colw),
            pl.BlockSpec((None, slab, D_MODEL), lambda i, j: (layer, i * nj + j, 0)),
        ],
        out_specs=[
            pl.BlockSpec((tm, tn), lambda i, j: (i, j)),
            pl.BlockSpec((slab, D_MODEL), lambda i, j: (i * nj + j, 0)),
        ],
        out_shape=[
            jax.ShapeDtypeStruct((n_tok, D_MODEL), BF16),
            jax.ShapeDtypeStruct((D_MODEL, D_MODEL), BF16),
        ],
        compiler_params=_params(("arbitrary", "arbitrary")),
        name="merge_branches",
    )(hq, hdq, ya, yb, yc, w_gl, w_gl, w_gl, w_gl_dq, w_gl_dq, w_gl_dq, wa, wb, wc, w_out)


def _proj_residual_kernel(a_ref, w_ref, x_ref, gt_ref, o_ref):
    o_ref[...] = x_ref[...] + gt_ref[0] * _dot(a_ref[...], w_ref[...])


def proj_residual(a, w, x2, gt, tm, tn, name):
    m, k = a.shape
    n = w.shape[-1]
    per_batch = SEQ // tm
    return pl.pallas_call(
        _proj_residual_kernel,
        grid=(m // tm, n // tn),
        in_specs=[
            pl.BlockSpec((tm, k), lambda i, j: (i, 0), pipeline_mode=RESIDENT),
            pl.BlockSpec((k, tn), lambda i, j: (0, j)),
            pl.BlockSpec((tm, tn), lambda i, j: (i, j)),
            pl.BlockSpec((1, 1, tn), lambda i, j: (i // per_batch, 0, j)),
        ],
        out_specs=pl.BlockSpec((tm, tn), lambda i, j: (i, j)),
        out_shape=jax.ShapeDtypeStruct((m, n), F32),
        compiler_params=_params(("arbitrary", "arbitrary")),
        name=name,
    )(a, w, x2, gt)


ADA_TILE = 256


def _ffn_up_kernel(a_ref, wg_ref, wu_ref, cw_ref, wd_ref, *rest, row_chunk, ada_steps):
    if ada_steps:
        c_ref, wada_ref, bada_ref, o_ref, wdb_ref, mod_ref = rest
        step = pl.program_id(0) * pl.num_programs(1) + pl.program_id(1)

        @pl.when(step < ada_steps)
        def _():
            mod_ref[...] = _ada_tile(c_ref, wada_ref, bada_ref)
    else:
        o_ref, wdb_ref = rest
    wdb_ref[...] = wd_ref[...].astype(wdb_ref.dtype)
    wg = wg_ref[...].astype(BF16)
    wu = wu_ref[...].astype(BF16)
    cw = cw_ref[...]
    prev = jnp.zeros((8, wg.shape[1]), F32)
    row = lax.broadcasted_iota(jnp.int32, (row_chunk, wg.shape[1]), 0)
    for r in range(a_ref.shape[0] // row_chunk):
        rows = slice(r * row_chunk, (r + 1) * row_chunk)
        a = a_ref[rows, :]
        g = _dot(a, wg)
        up = _dot(a, wu)
        s1 = jnp.where(row == 0, prev[7:8], pltpu.roll(g, 1, 0))
        s2 = jnp.where(row == 0, prev[6:7], jnp.where(row == 1, prev[7:8], pltpu.roll(g, 2, 0)))
        u = cw[2:3] * g + cw[1:2] * s1 + cw[0:1] * s2
        o_ref[rows, :] = (_silu(u) * up).astype(o_ref.dtype)
        prev = g[row_chunk - 8:row_chunk, :]


def ffn_up(h2, wg, wu, conv_w, w_down, layer, ada=None):
    tm, tn = SEQ, 256
    n_tok = h2.shape[0]
    ni, nj = n_tok // tm, D_FF // tn
    slab = D_FF // (ni * nj)
    assert slab * ni * nj == D_FF and slab % 16 == 0
    in_specs = [
        pl.BlockSpec((tm, D_MODEL), lambda i, j: (i, 0), pipeline_mode=RESIDENT),
        pl.BlockSpec((None, D_MODEL, tn), lambda i, j: (layer, 0, j)),
        pl.BlockSpec((None, D_MODEL, tn), lambda i, j: (layer, 0, j)),
        pl.BlockSpec((None, 3, tn), lambda i, j: (layer, 0, j)),
        pl.BlockSpec((None, slab, D_MODEL), lambda i, j: (layer, i * nj + j, 0)),
    ]
    out_specs = [
        pl.BlockSpec((tm, tn), lambda i, j: (i, j)),
        pl.BlockSpec((slab, D_MODEL), lambda i, j: (i * nj + j, 0)),
    ]
    out_shape = [
        jax.ShapeDtypeStruct((n_tok, D_FF), BF16),
        jax.ShapeDtypeStruct((D_FF, D_MODEL), BF16),
    ]
    operands = [h2, wg, wu, conv_w, w_down]
    ada_steps = 0
    if ada is not None:
        n_mod = ada[1].shape[-1]
        ada_steps = n_mod // ADA_TILE
        assert ada_steps * ADA_TILE == n_mod and ada_steps <= ni * nj
        tile = lambda i, j: jnp.minimum(i * nj + j, ada_steps - 1)
        in_specs += [
            pl.BlockSpec((8, D_MODEL), lambda i, j: (0, 0)),
            pl.BlockSpec((None, D_MODEL, ADA_TILE), lambda i, j: (layer + 1, 0, tile(i, j))),
            pl.BlockSpec((None, 1, ADA_TILE), lambda i, j: (layer + 1, 0, tile(i, j))),
        ]
        out_specs.append(pl.BlockSpec((8, ADA_TILE), lambda i, j: (0, tile(i, j))))
        out_shape.append(jax.ShapeDtypeStruct((8, n_mod), F32))
        operands += list(ada)
    return pl.pallas_call(
        functools.partial(_ffn_up_kernel, row_chunk=512, ada_steps=ada_steps),
        grid=(ni, nj),
        in_specs=in_specs,
        out_specs=out_specs,
        out_shape=out_shape,
        compiler_params=_params(("arbitrary", "arbitrary")),
        name="ffn_up",
    )(*operands)


def kernel(x, c, norm1_g, w_ada, b_ada, w_in, conv_w, gla_a2, gla_a_bias, gla_norm_g,
           w_branch_a, w_branch_b, w_branch_c, w_out, norm2_g, w_ffn_gate, w_ffn_up,
           ffn_conv_w, w_ffn_down, final_norm_g):
    batch = x.shape[0]
    n_tok = batch * SEQ
    slopes = alibi_slopes(MOBA_HEADS)
    c_pad = jnp.zeros((8, D_MODEL), F32).at[:batch].set(c)
    b_ada3 = b_ada.reshape(DEPTH, 1, b_ada.shape[-1])
    mod = ada_modulation(c_pad, w_ada, b_ada3, 0)
    w_in_t = jnp.swapaxes(w_in, 1, 2)

    for l in range(DEPTH):
        mod = mod[:batch].reshape(batch, 6, 1, D_MODEL)
        sh1, sc1, gt1, sh2, sc2, gt2 = (mod[:, t] for t in range(6))

        a2_pad = jnp.zeros((LANES, GLA_KW), F32).at[:GLA_RANK].set(gla_a2[l])

        h, hq, hdq = norm_modulate(x, norm1_g[l], sc1, sh1, with_fp8=True)
        h2 = h.reshape(n_tok, D_MODEL)
        z, ac, w_gl, w_gl_dq = in_proj_main(h2, w_in_t, l)
        z3 = z.reshape(batch, SEQ, Z_WIDTH)

        ya = moba_attention(z3, slopes)
        yb = gated_conv(z3, conv_w[l])
        yc = gla_mixer(z3, ac.reshape(batch, SEQ, LANES), a2_pad, gla_a_bias[l], gla_norm_g[l])

        merged, w_out_b = merge_branches(
            hq.reshape(n_tok, D_MODEL), hdq.reshape(n_tok, 1),
            ya.reshape(n_tok, MOBA_WIDTH), yb.reshape(n_tok, CONV_WIDTH), yc.reshape(n_tok, GLA_VW),
            w_gl, w_gl_dq, w_branch_a, w_branch_b, w_branch_c, w_out, l)
        x2 = proj_residual(merged, w_out_b, x.reshape(n_tok, D_MODEL), gt1, 2048, 512, "out_proj")
        x = x2.reshape(batch, SEQ, D_MODEL)

        h = norm_modulate(x, norm2_g[l], sc2, sh2)
        ada = (c_pad, w_ada, b_ada3) if l + 1 < DEPTH else None
        act, w_down_b, *mod_next = ffn_up(h.reshape(n_tok, D_MODEL), w_ffn_gate, w_ffn_up,
                                          ffn_conv_w, w_ffn_down, l, ada)
        if mod_next:
            mod = mod_next[0]
        x2 = proj_residual(act, w_down_b, x2, gt2, 1024, 512, "ffn_down")
        x = x2.reshape(batch, SEQ, D_MODEL)

    return final_norm(x, final_norm_g)
```
